```python
import functools
import jax, jax.numpy as jnp
from jax import lax
import numpy as np

D_MODEL = 1024
BATCH = 8
SEQ = 2048
DEPTH = 1
DEC_BATCH = 128
DEC_SEQ = 4
PAST_LEN = 8192
PAGE_SIZE = 128

CONV_DIM = 1024
CONV_WIDTH = 3
HEAD_DIM = 64
N_SLOTS = 8
GROUPS = ((128, 1), (512, 4), (2048, 16))
N_GROUPS = 3
ATTN_DIM = N_GROUPS * N_SLOTS * HEAD_DIM
ATTN_OUT = N_SLOTS * HEAD_DIM
Q_BLOCK = 128
PROJ_COLS = 3 * CONV_DIM + 3 * ATTN_DIM + 2 * D_MODEL
PEER_HEADS = 8
PEER_NKEYS = 128
PEER_EXPERTS = PEER_NKEYS * PEER_NKEYS
PEER_QDIM = 256
PEER_HALF = PEER_QDIM // 2
PEER_TOPK = 16
PEER_CHUNK = 256
EPS = 1e-6

kernel_name = 'hybrid_conv_dilated_attn_peer_step'


def _rmsnorm(x, g):
    xf = x.astype(jnp.float32)
    y = xf * lax.rsqrt(jnp.mean(xf * xf, axis=-1, keepdims=True) + EPS)
    return (y * g.astype(jnp.float32)).astype(x.dtype)


def _alibi_slopes():
    return jnp.exp2(-8.0 * jnp.arange(1, N_SLOTS + 1, dtype=jnp.float32) / N_SLOTS)


def _conv_branch(b_gate, c_gate, hv, prev, conv_w, w_out_a):
    u = c_gate * hv
    ext = jnp.concatenate([prev.astype(u.dtype), u], axis=1)
    t = u.shape[1]
    y = ext[:, 0:t] * conv_w[0]
    for i in range(1, CONV_WIDTH):
        y = y + ext[:, i:i + t] * conv_w[i]
    ya = (b_gate * y) @ w_out_a
    return ya, ext[:, ext.shape[1] - (CONV_WIDTH - 1):]


def _dilated_group(q, k_ext, v_ext, q_off, base_pos, dil, n_keys, slopes):
    tq = q.shape[1]
    t = jnp.arange(tq, dtype=jnp.int32)[:, None]
    i = jnp.arange(n_keys, dtype=jnp.int32)[None, :]
    j = q_off + t - dil * i
    valid = (j >= 0) & (base_pos + j >= 0)
    jc = jnp.clip(j, 0, k_ext.shape[1] - 1)
    kg = k_ext[:, jc]
    vg = v_ext[:, jc]
    dist = (dil * jnp.arange(n_keys)).astype(jnp.float32)
    s = jnp.einsum('bthd,btkhd->bthk', q, kg, preferred_element_type=jnp.float32)
    s = s * (HEAD_DIM ** -0.5) - slopes[:, None] * dist[None, :]
    s = jnp.where(valid[None, :, None, :], s, -jnp.inf)
    m = jnp.max(s, axis=-1, keepdims=True)
    p = jnp.exp(s - m)
    l = jnp.sum(p, axis=-1, keepdims=True)
    o = jnp.einsum('bthk,btkhd->bthd', p, vg.astype(jnp.float32)) / l
    lse = (m + jnp.log(l))[..., 0]
    return o, lse


def _combine(outs, lses):
    w = jax.nn.softmax(jnp.stack(lses, axis=0), axis=0)
    return jnp.sum(w[..., None] * jnp.stack(outs, axis=0), axis=0)


def _attn_prompt(q, k, v, slopes):
    bsz, seq = q.shape[:2]
    qs, kps, vps, bufs = [], [], [], []
    for g, (win, dil) in enumerate(GROUPS):
        kg, vg = k[:, :, g], v[:, :, g]
        pad = ((0, 0), (win, 0), (0, 0), (0, 0))
        qs.append(q[:, :, g])
        kps.append(jnp.pad(kg, pad))
        vps.append(jnp.pad(vg, pad))
        keep = min(win, seq)
        bufs.append(jnp.stack([kg[:, seq - keep:], vg[:, seq - keep:]], axis=2))

    def block(bi):
        t0 = bi * Q_BLOCK
        outs, lses = [], []
        for g, (win, dil) in enumerate(GROUPS):
            qb = lax.dynamic_slice_in_dim(qs[g], t0, Q_BLOCK, axis=1)
            kb = lax.dynamic_slice_in_dim(kps[g], t0, win + Q_BLOCK, axis=1)
            vb = lax.dynamic_slice_in_dim(vps[g], t0, win + Q_BLOCK, axis=1)
            o, lse = _dilated_group(qb, kb, vb, win, t0 - win, dil, win // dil + 1, slopes)
            outs.append(o)
            lses.append(lse)
        return _combine(outs, lses)

    o = lax.map(block, jnp.arange(seq // Q_BLOCK, dtype=jnp.int32))
    o = jnp.moveaxis(o, 0, 1).reshape(bsz, seq, N_SLOTS, HEAD_DIM)
    return o, bufs


def _attn_sample(q, k, v, bufs, slopes):
    outs, lses, new_bufs = [], [], []
    for g, (win, dil) in enumerate(GROUPS):
        buf = bufs[g]
        wb = buf.shape[1]
        k_ext = jnp.concatenate([buf[:, :, 0].astype(k.dtype), k[:, :, g]], axis=1)
        v_ext = jnp.concatenate([buf[:, :, 1].astype(v.dtype), v[:, :, g]], axis=1)
        o, lse = _dilated_group(q[:, :, g], k_ext, v_ext, wb, PAST_LEN - wb, dil, win // dil + 1, slopes)
        outs.append(o)
        lses.append(lse)
        n = k_ext.shape[1]
        new_bufs.append(jnp.stack([k_ext[:, n - wb:], v_ext[:, n - wb:]], axis=2))
    return _combine(outs, lses), new_bufs


def _peer(h, wq, keys, u_tab, v_tab):
    shp = h.shape
    flat = h.reshape(-1, D_MODEL)
    n = flat.shape[0]
    flat = jnp.pad(flat, ((0, (-n) % PEER_CHUNK), (0, 0)))
    chunks = flat.reshape(-1, PEER_CHUNK, D_MODEL)
    k1 = keys[0].astype(jnp.float32)
    k2 = keys[1].astype(jnp.float32)

    def body(hc):
        qh = (hc @ wq).astype(jnp.float32).reshape(PEER_CHUNK, PEER_HEADS, 2, PEER_HALF)
        s1 = jnp.einsum('chd,hnd->chn', qh[:, :, 0], k1)
        s2 = jnp.einsum('chd,hnd->chn', qh[:, :, 1], k2)
        v1, i1 = lax.top_k(s1, PEER_TOPK)
        v2, i2 = lax.top_k(s2, PEER_TOPK)
        cand = (v1[..., :, None] + v2[..., None, :]).reshape(PEER_CHUNK, PEER_HEADS, PEER_TOPK * PEER_TOPK)
        cidx = (i1[..., :, None] * PEER_NKEYS + i2[..., None, :]).reshape(PEER_CHUNK, PEER_HEADS, PEER_TOPK * PEER_TOPK)
        sv, si = lax.top_k(cand, PEER_TOPK)
        eidx = jnp.take_along_axis(cidx, si, axis=-1)
        gate = jax.nn.softmax(sv, axis=-1)
        act = jax.nn.gelu(jnp.einsum('cd,chkd->chk', hc, u_tab[eidx]).astype(jnp.float32), approximate=False)
        out = jnp.einsum('chk,chkd->cd', gate * act, v_tab[eidx].astype(jnp.float32))
        return out.astype(hc.dtype)

    out = lax.map(body, chunks).reshape(-1, D_MODEL)[:n]
    return out.reshape(shp)


def _layer(x, conv_prev, attend, g1, w_in, conv_w, w_out_a, w_out_b, w_o,
           g2, peer_wq, peer_keys, peer_u, peer_v):
    bsz, t, _ = x.shape
    h = _rmsnorm(x, g1)
    p = h @ w_in
    splits = [CONV_DIM, 2 * CONV_DIM, 3 * CONV_DIM,
              3 * CONV_DIM + ATTN_DIM, 3 * CONV_DIM + 2 * ATTN_DIM,
              3 * CONV_DIM + 3 * ATTN_DIM, 3 * CONV_DIM + 3 * ATTN_DIM + D_MODEL]
    b_gate, c_gate, hv, q, k, v, ga, gb = jnp.split(p, splits, axis=-1)
    ya, conv_new = _conv_branch(b_gate, c_gate, hv, conv_prev, conv_w, w_out_a)
    shp5 = (bsz, t, N_GROUPS, N_SLOTS, HEAD_DIM)
    ob, kv_new = attend(q.reshape(shp5), k.reshape(shp5), v.reshape(shp5))
    yb = ob.reshape(bsz, t, ATTN_OUT).astype(x.dtype) @ w_out_b
    x = x + (jax.nn.sigmoid(ga) * ya + jax.nn.sigmoid(gb) * yb) @ w_o
    x = x + _peer(_rmsnorm(x, g2), peer_wq, peer_keys, peer_u, peer_v)
    return x, kv_new, conv_new


def setup_inputs(seed: int = 0) -> dict:
    key = jax.random.key(seed)
    ks = jax.random.split(key, 20)

    def nrm(k, shape, scale):
        return jax.random.normal(k, shape, jnp.float32) * scale

    def kvbuf(k, win):
        return nrm(k, (DEPTH, DEC_BATCH, min(win, PAST_LEN), 2, N_SLOTS, HEAD_DIM), 1.0)

    return {
        'x_prompt': nrm(ks[0], (BATCH, SEQ, D_MODEL), 1.0),
        'x_sample': nrm(ks[1], (DEC_BATCH, DEC_SEQ, D_MODEL), 1.0),
        'cache_kv_w128': kvbuf(ks[2], GROUPS[0][0]),
        'cache_kv_w512': kvbuf(ks[3], GROUPS[1][0]),
        'cache_kv_w2048': kvbuf(ks[4], GROUPS[2][0]),
        'state_conv': nrm(ks[5], (DEPTH, DEC_BATCH, CONV_WIDTH - 1, CONV_DIM), 1.0),
        'norm1_g': 1.0 + nrm(ks[6], (DEPTH, D_MODEL), 0.02),
        'w_in': nrm(ks[7], (DEPTH, D_MODEL, PROJ_COLS), D_MODEL ** -0.5),
        'conv_w': nrm(ks[8], (DEPTH, CONV_WIDTH, CONV_DIM), CONV_WIDTH ** -0.5),
        'w_out_a': nrm(ks[9], (DEPTH, CONV_DIM, D_MODEL), CONV_DIM ** -0.5),
        'w_out_b': nrm(ks[10], (DEPTH, ATTN_OUT, D_MODEL), ATTN_OUT ** -0.5),
        'w_o': nrm(ks[11], (DEPTH, D_MODEL, D_MODEL), D_MODEL ** -0.5),
        'norm2_g': 1.0 + nrm(ks[12], (DEPTH, D_MODEL), 0.02),
        'peer_wq': nrm(ks[13], (DEPTH, D_MODEL, PEER_HEADS * PEER_QDIM), D_MODEL ** -0.5),
        'peer_keys': nrm(ks[14], (DEPTH, 2, PEER_HEADS, PEER_NKEYS, PEER_HALF), PEER_HALF ** -0.5),
        'peer_u': nrm(ks[15], (DEPTH, PEER_EXPERTS, D_MODEL), D_MODEL ** -0.5),
        'peer_v': nrm(ks[16], (DEPTH, PEER_EXPERTS, D_MODEL), 0.3),
        'final_g': 1.0 + nrm(ks[17], (D_MODEL,), 0.02),
    }


def reference(x_prompt, x_sample, cache_kv_w128, cache_kv_w512, cache_kv_w2048, state_conv,
              norm1_g, w_in, conv_w, w_out_a, w_out_b, w_o, norm2_g,
              peer_wq, peer_keys, peer_u, peer_v, final_g):
    slopes = _alibi_slopes()
    attend_prompt = functools.partial(_attn_prompt, slopes=slopes)
    yp, ys = x_prompt, x_sample
    kvp = ([], [], [])
    kvs = ([], [], [])
    convp, convs = [], []
    for l in range(DEPTH):
        w = (norm1_g[l], w_in[l], conv_w[l], w_out_a[l], w_out_b[l], w_o[l],
             norm2_g[l], peer_wq[l], peer_keys[l], peer_u[l], peer_v[l])
        prev0 = jnp.zeros((yp.shape[0], CONV_WIDTH - 1, CONV_DIM), yp.dtype)
        yp, kv_p, c_p = _layer(yp, prev0, attend_prompt, *w)
        bufs = (cache_kv_w128[l], cache_kv_w512[l], cache_kv_w2048[l])
        attend_sample = functools.partial(_attn_sample, bufs=bufs, slopes=slopes)
        ys, kv_s, c_s = _layer(ys, state_conv[l], attend_sample, *w)
        for g in range(N_GROUPS):
            kvp[g].append(kv_p[g])
            kvs[g].append(kv_s[g])
        convp.append(c_p)
        convs.append(c_s)
    y_prompt = _rmsnorm(yp, final_g)
    y_sample = _rmsnorm(ys, final_g)
    return (y_prompt, y_sample,
            jnp.stack(kvp[0], 0), jnp.stack(kvp[1], 0), jnp.stack(kvp[2], 0), jnp.stack(convp, 0),
            jnp.stack(kvs[0], 0), jnp.stack(kvs[1], 0), jnp.stack(kvs[2], 0), jnp.stack(convs, 0))
```

```python
import functools

import jax
import jax.numpy as jnp
import numpy as np
from jax import lax
from jax.experimental import pallas as pl
from jax.experimental.pallas import tpu as pltpu

F32 = jnp.float32
BF16 = jnp.bfloat16

HEAD_DIM = 64
N_SLOTS = 8
GROUPS = ((128, 1), (512, 4), (2048, 16))
N_GROUPS = len(GROUPS)
ATTN_OUT = N_SLOTS * HEAD_DIM
WIN = 128
CONV_WIDTH = 3
PEER_TOPK = 16
PEER_NKEYS = 128
EPS = 1e-6
NEG = -1e30

LANES = 128
VMEM_LIMIT = 56 * 1024 * 1024

_NT = (((1,), (1,)), ((), ()))


def _params(sem):
    return pltpu.CompilerParams(dimension_semantics=sem, vmem_limit_bytes=VMEM_LIMIT)


def _resident():
    return pl.BlockSpec(memory_space=pltpu.VMEM)


def _rms(x, g):
    return x * lax.rsqrt(jnp.mean(x * x, axis=-1, keepdims=True) + EPS) * g


def _slopes():
    return np.exp2(-8.0 * np.arange(1, N_SLOTS + 1, dtype=np.float64) / N_SLOTS)


def _prep_body(u_ref, v_ref, ub_ref, vt_ref):
    ub_ref[...] = u_ref[...].astype(BF16)
    vt_ref[...] = v_ref[...].T.astype(BF16)


def _prep_tables(u, v):
    e, d = u.shape
    blk = 512
    return pl.pallas_call(
        _prep_body,
        grid=(e // blk,),
        in_specs=[pl.BlockSpec((blk, d), lambda i: (i, 0)), pl.BlockSpec((blk, d), lambda i: (i, 0))],
        out_specs=[pl.BlockSpec((blk, d), lambda i: (i, 0)), pl.BlockSpec((d, blk), lambda i: (0, i))],
        out_shape=[jax.ShapeDtypeStruct((e, d), BF16), jax.ShapeDtypeStruct((d, e), BF16)],
        compiler_params=_params(("arbitrary",)),
        name="prep_tables",
    )(u, v)


def _proj_core(h, win_ref, woa_ref, conv_fn, c, d, q_ref, kv_refs, gya_ref, sgb_ref):
    def proj(a, n):
        return jnp.dot(h, win_ref[:, a:a + n], preferred_element_type=F32)

    o_q = 3 * c
    o_kv = o_q + N_GROUPS * ATTN_OUT
    o_ga = o_kv + N_GROUPS * 2 * ATTN_OUT
    o_gb = o_ga + d
    u = proj(c, c) * proj(2 * c, c)
    y = conv_fn(u)
    ya = jnp.dot((proj(0, c) * y).astype(BF16), woa_ref[...], preferred_element_type=F32)
    gya_ref[...] = jax.nn.sigmoid(proj(o_ga, d)) * ya
    sgb_ref[...] = jax.nn.sigmoid(proj(o_gb, d))
    q_ref[...] = proj(o_q, N_GROUPS * ATTN_OUT) * (HEAD_DIM ** -0.5)
    for g in range(N_GROUPS):
        kv_refs[g][...] = proj(o_kv + g * 2 * ATTN_OUT, 2 * ATTN_OUT)


def _proj_prompt_body(x_ref, g1_ref, win_ref, cw_ref, woa_ref,
                      q_ref, kv0_ref, kv1_ref, kv2_ref, gya_ref, sgb_ref, cst_ref, uext_ref, *, tm, c, d):
    j = pl.program_id(1)
    pre = 8

    @pl.when(j == 0)
    def _():
        uext_ref[0:pre, :] = jnp.zeros((pre, c), F32)

    def conv(u):
        uext_ref[pre:pre + tm, :] = u
        cw = cw_ref[...]
        return (cw[0:1] * uext_ref[pre - 2:pre - 2 + tm, :]
                + cw[1:2] * uext_ref[pre - 1:pre - 1 + tm, :]
                + cw[2:3] * u)

    h = _rms(x_ref[...], g1_ref[...]).astype(BF16)
    _proj_core(h, win_ref, woa_ref, conv, c, d, q_ref, (kv0_ref, kv1_ref, kv2_ref), gya_ref, sgb_ref)
    cst_ref[...] = uext_ref[pre + tm - 2:pre + tm, :]
    uext_ref[0:pre, :] = uext_ref[tm:tm + pre, :]


def _proj_prompt(x, g1, win, cw, woa, tm=256):
    b, s, d = x.shape
    c = cw.shape[1]
    nq = N_GROUPS * ATTN_OUT
    tok = lambda n: pl.BlockSpec((None, tm, n), lambda i, j: (i, j, 0))
    outs = [jax.ShapeDtypeStruct((b, s, nq), F32)]
    outs += [jax.ShapeDtypeStruct((b, s, 2 * ATTN_OUT), F32)] * N_GROUPS
    outs += [jax.ShapeDtypeStruct((b, s, d), F32)] * 2
    outs += [jax.ShapeDtypeStruct((b, CONV_WIDTH - 1, c), F32)]
    return pl.pallas_call(
        functools.partial(_proj_prompt_body, tm=tm, c=c, d=d),
        grid=(b, s // tm),
        in_specs=[tok(d), _resident(), _resident(), _resident(), _resident()],
        out_specs=[tok(nq)] + [tok(2 * ATTN_OUT)] * N_GROUPS + [tok(d), tok(d),
                   pl.BlockSpec((None, CONV_WIDTH - 1, c), lambda i, j: (i, 0, 0))],
        out_shape=outs,
        scratch_shapes=[pltpu.VMEM((tm + 8, c), F32)],
        compiler_params=_params(("arbitrary", "arbitrary")),
        name="proj_prompt",
    )(x, g1, win, cw, woa)


def _proj_sample_body(x_ref, st_ref, g1_ref, win_ref, cw_ref, woa_ref,
                      q_ref, kv0_ref, kv1_ref, kv2_ref, gya_ref, sgb_ref, cst_ref, uext_ref, *, nb, nt, c, d):
    npre = (CONV_WIDTH - 1) * nb

    def conv(u):
        uext_ref[0:npre, :] = st_ref[...]
        uext_ref[npre:npre + nt * nb, :] = u
        cw = cw_ref[...]
        return (cw[0:1] * uext_ref[0:nt * nb, :]
                + cw[1:2] * uext_ref[nb:nb + nt * nb, :]
                + cw[2:3] * u)

    h = _rms(x_ref[...], g1_ref[...]).astype(BF16)
    _proj_core(h, win_ref, woa_ref, conv, c, d, q_ref, (kv0_ref, kv1_ref, kv2_ref), gya_ref, sgb_ref)
    cst_ref[...] = uext_ref[nt * nb:nt * nb + npre, :]


def _proj_sample(x_tm, st_tm, g1, win, cw, woa, nb, nt):
    n, d = x_tm.shape
    c = cw.shape[1]
    nq = N_GROUPS * ATTN_OUT
    npre = (CONV_WIDTH - 1) * nb
    outs = [jax.ShapeDtypeStruct((n, nq), F32)]
    outs += [jax.ShapeDtypeStruct((n, 2 * ATTN_OUT), F32)] * N_GROUPS
    outs += [jax.ShapeDtypeStruct((n, d), F32)] * 2
    outs += [jax.ShapeDtypeStruct((npre, c), F32)]
    return pl.pallas_call(
        functools.partial(_proj_sample_body, nb=nb, nt=nt, c=c, d=d),
        in_specs=[_resident()] * 6,
        out_specs=[_resident()] * 7,
        out_shape=outs,
        scratch_shapes=[pltpu.VMEM((npre + n, c), F32)],
        compiler_params=pltpu.CompilerParams(vmem_limit_bytes=VMEM_LIMIT),
        name="proj_sample",
    )(x_tm, st_tm, g1, win, cw, woa)


def _prompt_bias(dil):
    tq = np.arange(WIN)[:, None]
    col = np.arange(2 * WIN)[None, :]
    dist = tq + WIN - col
    ok = (dist >= 0) & (dist <= WIN)
    b = -_slopes()[:, None, None] * (dil * dist)[None]
    return jnp.asarray(np.where(ok[None], b, NEG), F32)


def _attn_prompt_body(q_ref, kvc_ref, kvp_ref, bias_ref, acc_ref, m_ref, l_ref):
    j = pl.program_id(2)
    lo = lax.broadcasted_iota(jnp.int32, (WIN, LANES), 1) < HEAD_DIM
    col = lax.broadcasted_iota(jnp.int32, (WIN, 2 * WIN), 1)
    kill = jnp.logical_and(col < WIN, j == 0)
    for p in range(ATTN_OUT // LANES):
        ls = slice(p * LANES, (p + 1) * LANES)
        vs = slice(ATTN_OUT + p * LANES, ATTN_OUT + (p + 1) * LANES)
        qp = q_ref[:, ls]
        kcat = jnp.concatenate([kvp_ref[:, ls], kvc_ref[:, ls]], axis=0).astype(BF16)
        vcat = jnp.concatenate([kvp_ref[:, vs], kvc_ref[:, vs]], axis=0).astype(BF16)
        res = []
        for hh in range(2):
            qm = jnp.where(lo if hh == 0 else jnp.logical_not(lo), qp, 0.0).astype(BF16)
            s = lax.dot_general(qm, kcat, _NT, preferred_element_type=F32)
            s = jnp.where(kill, NEG, s + bias_ref[2 * p + hh])
            m = jnp.max(s, axis=-1, keepdims=True)
            e = jnp.exp(s - m)
            l = jnp.sum(e, axis=-1, keepdims=True)
            pv = jnp.dot(e.astype(BF16), vcat, preferred_element_type=F32)
            res.append((m, l, pv))
        acc_ref[:, ls] = jnp.where(lo, res[0][2], res[1][2])
        m_ref[:, ls] = jnp.where(lo, res[0][0], res[1][0])
        l_ref[:, ls] = jnp.where(lo, res[0][1], res[1][1])


def _attn_prompt_group(q, kv, g):
    b, s, nq = q.shape
    dil = GROUPS[g][1]
    sr = s // dil
    nblk = sr // WIN
    qv = q.reshape(b, sr, dil * nq)
    kvv = kv.reshape(b, sr, dil * 2 * ATTN_OUT)
    nqb = nq // ATTN_OUT
    out_sd = jax.ShapeDtypeStruct((b, sr, dil * ATTN_OUT), F32)
    o_spec = pl.BlockSpec((None, WIN, ATTN_OUT), lambda i, r, j: (i, j, r))
    acc, m, l = pl.pallas_call(
        _attn_prompt_body,
        grid=(b, dil, nblk),
        in_specs=[
            pl.BlockSpec((None, WIN, ATTN_OUT), lambda i, r, j: (i, j, r * nqb + g)),
            pl.BlockSpec((None, WIN, 2 * ATTN_OUT), lambda i, r, j: (i, j, r)),
            pl.BlockSpec((None, WIN, 2 * ATTN_OUT), lambda i, r, j: (i, jnp.maximum(j - 1, 0), r)),
            _resident(),
        ],
        out_specs=[o_spec, o_spec, o_spec],
        out_shape=[out_sd, out_sd, out_sd],
        compiler_params=_params(("arbitrary", "arbitrary", "arbitrary")),
        name=f"attn_prompt_g{g}",
    )(qv, kvv, kvv, _prompt_bias(dil))
    shp = (b * s, ATTN_OUT)
    return acc.reshape(shp), m.reshape(shp), l.reshape(shp)


def _bf16_round(x):
    return x.astype(BF16).astype(F32)


def _attn_sample_body(q_ref, n0_ref, n1_ref, n2_ref, c0_ref, c1_ref, c2_ref,
                      b0_ref, b12_ref, slope_ref, out_ref, *, bb, nt):
    t = pl.program_id(1)
    lane = lax.broadcasted_iota(jnp.int32, (N_SLOTS, ATTN_OUT), 1)
    row = lax.broadcasted_iota(jnp.int32, (N_SLOTS, ATTN_OUT), 0)
    headmask = (lane // HEAD_DIM) == row
    slope = slope_ref[...]
    new_refs = (n0_ref, n1_ref, n2_ref)
    cache_refs = (c0_ref, c1_ref, c2_ref)

    def per_seq(b, carry):
        qrow = q_ref[b]
        stats = []
        for g in range(N_GROUPS):
            qexp = jnp.where(headmask, qrow[:, g * ATTN_OUT:(g + 1) * ATTN_OUT], 0.0)
            qexp_r = _bf16_round(qexp)
            kc = cache_refs[g][b]
            s = lax.dot_general(qexp.astype(BF16), kc[:, :ATTN_OUT].astype(BF16), _NT,
                                preferred_element_type=F32)
            s = s + (b0_ref[t] if g == 0 else b12_ref[g - 1])
            news = []
            for tn in (range(nt) if g == 0 else (None,)):
                tt = t if tn is None else tn
                kvn = new_refs[g][tt, b]
                sn = jnp.sum(qexp_r * _bf16_round(kvn[:, :ATTN_OUT]), axis=-1, keepdims=True)
                if tn is not None:
                    sn = jnp.where(tn <= t, sn - slope * (t - tn).astype(F32), NEG)
                news.append((sn, kvn[:, ATTN_OUT:]))
            m = jnp.max(s, axis=-1, keepdims=True)
            for sn, _ in news:
                m = jnp.maximum(m, sn)
            e = jnp.exp(s - m)
            l = jnp.sum(e, axis=-1, keepdims=True)
            pv = jnp.dot(e.astype(BF16), kc[:, ATTN_OUT:].astype(BF16), preferred_element_type=F32)
            for sn, vn in news:
                en = jnp.exp(sn - m)
                l = l + en
                pv = pv + _bf16_round(en) * _bf16_round(vn)
            stats.append((m, l, pv))
        mm = functools.reduce(jnp.maximum, [st[0] for st in stats])
        num = 0.0
        den = 0.0
        for m, l, pv in stats:
            w = jnp.exp(m - mm)
            num = num + w * pv
            den = den + w * l
        o = jnp.sum(jnp.where(headmask, num / den, 0.0), axis=0, keepdims=True)
        out_ref[b] = o
        return carry

    lax.fori_loop(0, bb, per_seq, 0)


def _attn_sample(q_tm, news_tm, caches, nb, nt, bb=8):
    sl = _slopes()
    nq = N_GROUPS * ATTN_OUT
    cviews, cspecs = [], []
    for g, (win, dil) in enumerate(GROUPS):
        wb = caches[g].shape[1]
        assert wb == win, "window buffer must hold a full window"
        cviews.append(caches[g].reshape(nb, WIN, dil * 2 * ATTN_OUT))
        if g == 0:
            cspecs.append(pl.BlockSpec((bb, WIN, 2 * ATTN_OUT), lambda i, t: (i, 0, 0)))
        else:
            assert nt <= dil
            cspecs.append(pl.BlockSpec((bb, WIN, 2 * ATTN_OUT), lambda i, t: (i, 0, t)))
    r = np.arange(WIN)
    b0 = np.stack([np.where(r[None, :] >= t, -sl[:, None] * (WIN + t - r)[None, :], NEG) for t in range(nt)])
    b12 = np.stack([-sl[:, None] * (GROUPS[g][1] * (WIN - r))[None, :] for g in (1, 2)])
    nspec = pl.BlockSpec((nt, bb, 1, 2 * ATTN_OUT), lambda i, t: (0, i, 0, 0))
    return pl.pallas_call(
        functools.partial(_attn_sample_body, bb=bb, nt=nt),
        grid=(nb // bb, nt),
        in_specs=[pl.BlockSpec((None, bb, 1, nq), lambda i, t: (t, i, 0, 0)), nspec, nspec, nspec] + cspecs
                 + [_resident(), _resident(), _resident()],
        out_specs=pl.BlockSpec((None, bb, 1, ATTN_OUT), lambda i, t: (t, i, 0, 0)),
        out_shape=jax.ShapeDtypeStruct((nt, nb, 1, ATTN_OUT), F32),
        compiler_params=_params(("arbitrary", "arbitrary")),
        name="attn_sample",
    )(q_tm[:, :, None, :], *[x[:, :, None, :] for x in news_tm], *cviews,
      jnp.asarray(b0, F32), jnp.asarray(b12, F32), jnp.asarray(sl[:, None], F32))


def _cache_copies(refs, sem, nt, nchunk):
    ng = N_GROUPS
    caches, news, outs = refs[:ng], refs[ng:2 * ng], refs[2 * ng:]
    copies = []
    k = 0
    for g in range(ng):
        nb, wb = caches[g].shape[:2]
        cb = nb // nchunk
        for i in range(nchunk):
            bs = pl.ds(i * cb, cb)
            copies.append(pltpu.make_async_copy(caches[g].at[bs, pl.ds(nt, wb - nt)],
                                                outs[g].at[bs, pl.ds(0, wb - nt)], sem.at[k]))
            k += 1
        for t in range(nt):
            copies.append(pltpu.make_async_copy(news[g].at[t], outs[g].at[:, wb - nt + t], sem.at[k]))
            k += 1
    return copies


def _cache_body(*refs, nt, nchunk):
    copies = _cache_copies(refs[:-1], refs[-1], nt, nchunk)
    for cp in copies:
        cp.start()
    for cp in copies:
        cp.wait()


def _cache_update(caches, news_tm, nt, nchunk=8):
    for c in caches:
        assert c.shape[1] > nt and c.shape[0] % nchunk == 0
    nsem = N_GROUPS * (nchunk + nt)
    anyspec = pl.BlockSpec(memory_space=pl.ANY)
    return pl.pallas_call(
        functools.partial(_cache_body, nt=nt, nchunk=nchunk),
        in_specs=[anyspec] * (2 * N_GROUPS),
        out_specs=[anyspec] * N_GROUPS,
        out_shape=[jax.ShapeDtypeStruct(c.shape, c.dtype) for c in caches],
        scratch_shapes=[pltpu.SemaphoreType.DMA((nsem,))],
        name="cache_update",
    )(*caches, *news_tm)


def _mix_body(*refs, combine):
    if combine:
        stats = [refs[3 * g:3 * g + 3] for g in range(N_GROUPS)]
        rest = refs[3 * N_GROUPS:]
        mm = functools.reduce(jnp.maximum, [st[1][...] for st in stats])
        num = 0.0
        den = 0.0
        for a_ref, m_ref, l_ref in stats:
            w = jnp.exp(m_ref[...] - mm)
            num = num + w * a_ref[...]
            den = den + w * l_ref[...]
        ob = num / den
    else:
        ob = refs[0][...]
        rest = refs[1:]
    gya_ref, sgb_ref, x_ref, wob_ref, wo_ref, g2_ref, x1_ref, h2_ref, h2t_ref = rest
    yb = jnp.dot(ob.astype(BF16), wob_ref[...], preferred_element_type=F32)
    mix = gya_ref[...] + sgb_ref[...] * yb
    x1 = x_ref[...] + jnp.dot(mix.astype(BF16), wo_ref[...], preferred_element_type=F32)
    x1_ref[...] = x1
    h2 = _rms(x1, g2_ref[...])
    h2_ref[...] = h2.astype(BF16)
    h2t_ref[...] = h2.T.astype(BF16)


def _mix(attn, gya, sgb, x, wob, wo, g2, tm=512):
    n, d = x.shape
    tm = min(tm, n)
    combine = len(attn) > 1
    tok = lambda w: pl.BlockSpec((tm, w), lambda i: (i, 0))
    return pl.pallas_call(
        functools.partial(_mix_body, combine=combine),
        grid=(n // tm,),
        in_specs=[tok(ATTN_OUT)] * len(attn) + [tok(d), tok(d), tok(d), _resident(), _resident(), _resident()],
        out_specs=[tok(d), tok(d), pl.BlockSpec((d, tm), lambda i: (0, i))],
        out_shape=[jax.ShapeDtypeStruct((n, d), F32), jax.ShapeDtypeStruct((n, d), BF16),
                   jax.ShapeDtypeStruct((d, n), BF16)],
        compiler_params=_params(("arbitrary",)),
        name="mix_combine" if combine else "mix",
    )(*attn, gya, sgb, x, wob, wo, g2)


def _extract_top(work, k):
    nrow = work.shape[0]
    iota = lax.broadcasted_iota(jnp.int32, work.shape, 0)
    vals = []
    for _ in range(k):
        mx = jnp.max(work, axis=0, keepdims=True)
        vals.append(mx)
        first = jnp.min(jnp.where(work == mx, iota, nrow), axis=0, keepdims=True)
        work = jnp.where(iota == first, -jnp.inf, work)
    return vals


def _route_body(h2_ref, wq_ref, keys_ref, s_ref, ec_ref, tau_ref, vals_ref, *, tm, nh):
    q = jnp.dot(h2_ref[...], wq_ref[...], preferred_element_type=F32).astype(BF16)
    for hh in range(2 * nh):
        s_ref[hh] = lax.dot_general(keys_ref[hh], q[:, hh * PEER_NKEYS:(hh + 1) * PEER_NKEYS], _NT,
                                    preferred_element_type=F32)
    nchunk = tm // LANES

    def top_keys(i, carry):
        hh = i // nchunk
        cs = pl.ds(pl.multiple_of((i % nchunk) * LANES, LANES), LANES)
        vals = _extract_top(s_ref[hh, :, cs], PEER_TOPK)
        for k, v in enumerate(vals):
            vals_ref[hh, k:k + 1, cs] = v
        return carry

    lax.fori_loop(0, 2 * nh * nchunk, top_keys, 0)

    r8 = lax.broadcasted_iota(jnp.int32, (8, LANES), 0)

    def top_pairs(i, carry):
        h = i // nchunk
        cs = pl.ds(pl.multiple_of((i % nchunk) * LANES, LANES), LANES)
        v1 = vals_ref[2 * h, :, cs]
        v2 = vals_ref[2 * h + 1, :, cs]
        blocks = [v1[0:1] + v2]
        for a in range(1, 8):
            nb_ = PEER_TOPK // (a + 1)
            blk = v1[a:a + 1] + v2[0:8]
            blocks.append(blk if nb_ >= 8 else jnp.where(r8 < nb_, blk, -jnp.inf))
        blocks.append(v1[8:16] + v2[0:1])
        sv = _extract_top(jnp.concatenate(blocks, axis=0), PEER_TOPK)
        z = functools.reduce(lambda a, b: a + b, [jnp.exp(v - sv[0]) for v in sv])
        tau_ref[h, :, cs] = sv[PEER_TOPK - 1]
        ec_ref[2 * h, :, cs] = jnp.exp(s_ref[2 * h, :, cs] - v1[0:1]) / z
        ec_ref[2 * h + 1, :, cs] = jnp.exp(s_ref[2 * h + 1, :, cs] - v2[0:1])
        return carry

    lax.fori_loop(0, nh * nchunk, top_pairs, 0)


def _route(h2, wq, keys, tm=512):
    n, d = h2.shape
    nh2 = keys.shape[0]
    tm = min(tm, n)
    big = pl.BlockSpec((nh2, PEER_NKEYS, tm), lambda i: (0, 0, i))
    return pl.pallas_call(
        functools.partial(_route_body, tm=tm, nh=nh2 // 2),
        grid=(n // tm,),
        in_specs=[pl.BlockSpec((tm, d), lambda i: (i, 0)), _resident(), _resident()],
        out_specs=[big, big, pl.BlockSpec((nh2 // 2, 1, tm), lambda i: (0, 0, i))],
        out_shape=[jax.ShapeDtypeStruct((nh2, PEER_NKEYS, n), F32), jax.ShapeDtypeStruct((nh2, PEER_NKEYS, n), F32),
                   jax.ShapeDtypeStruct((nh2 // 2, 1, n), F32)],
        scratch_shapes=[pltpu.VMEM((nh2, PEER_TOPK, tm), F32)],
        compiler_params=_params(("arbitrary",)),
        name="peer_route",
    )(h2, wq, keys)


def _experts_body(h2t_ref, u_ref, vt_ref, s_ref, ec_ref, tau_ref, x1_ref, gf_ref, y_ref,
                  acc_ref, act_ref, p_ref, *, tm, eblk, nh):
    j = pl.program_id(1)
    nchunk = tm // LANES
    nsub = eblk // PEER_NKEYS

    @pl.when(j == 0)
    def _():
        acc_ref[...] = jnp.zeros_like(acc_ref)

    act_ref[...] = jnp.dot(u_ref[...], h2t_ref[...], preferred_element_type=F32)
    r8 = lax.broadcasted_iota(jnp.int32, (8, LANES), 0)

    def gate(i, carry):
        sub = i // nchunk
        cs = pl.ds(pl.multiple_of((i % nchunk) * LANES, LANES), LANES)
        rs = pl.ds(pl.multiple_of(sub * PEER_NKEYS, PEER_NKEYS), PEER_NKEYS)
        i1 = j * nsub + sub
        grp = pl.ds(pl.multiple_of((i1 // 8) * 8, 8), 8)
        pick = r8 == (i1 % 8)
        g = jnp.zeros((PEER_NKEYS, LANES), F32)
        for h in range(nh):
            s1 = jnp.sum(jnp.where(pick, s_ref[2 * h, grp, cs], 0.0), axis=0, keepdims=True)
            c1 = jnp.sum(jnp.where(pick, ec_ref[2 * h, grp, cs], 0.0), axis=0, keepdims=True)
            pair = s_ref[2 * h + 1, :, cs] + s1
            w = ec_ref[2 * h + 1, :, cs] * c1
            g = g + jnp.where(pair >= tau_ref[h, :, cs], w, 0.0)
        a = act_ref[rs, cs]
        gelu = 0.5 * a * (1.0 + lax.erf(a * np.float32(np.sqrt(0.5))))
        p_ref[rs, cs] = (g * gelu).astype(BF16)
        return carry

    lax.fori_loop(0, nsub * nchunk, gate, 0)
    acc_ref[...] += jnp.dot(vt_ref[...], p_ref[...], preferred_element_type=F32)

    @pl.when(j == pl.num_programs(1) - 1)
    def _():
        y_ref[...] = _rms(x1_ref[...] + acc_ref[...].T, gf_ref[...])


def _experts(h2t, ub, vtb, s_all, ec_all, tau, x1, gf, tm=512, eblk=512):
    d, n = h2t.shape
    e = ub.shape[0]
    nh2 = s_all.shape[0]
    tm = min(tm, n)
    big = pl.BlockSpec((nh2, PEER_NKEYS, tm), lambda i, j: (0, 0, i))
    return pl.pallas_call(
        functools.partial(_experts_body, tm=tm, eblk=eblk, nh=nh2 // 2),
        grid=(n // tm, e // eblk),
        in_specs=[
            pl.BlockSpec((d, tm), lambda i, j: (0, i)),
            pl.BlockSpec((eblk, d), lambda i, j: (j, 0)),
            pl.BlockSpec((d, eblk), lambda i, j: (0, j)),
            big, big,
            pl.BlockSpec((nh2 // 2, 1, tm), lambda i, j: (0, 0, i)),
            pl.BlockSpec((tm, d), lambda i, j: (i, 0)),
            _resident(),
        ],
        out_specs=pl.BlockSpec((tm, d), lambda i, j: (i, 0)),
        out_shape=jax.ShapeDtypeStruct((n, d), F32),
        scratch_shapes=[pltpu.VMEM((d, tm), F32), pltpu.VMEM((eblk, tm), F32), pltpu.VMEM((eblk, tm), BF16)],
        compiler_params=_params(("arbitrary", "arbitrary")),
        name="peer_experts",
    )(h2t, ub, vtb, s_all, ec_all, tau, x1, gf)


def _arrange_w_in(w, c, d):
    na = N_GROUPS * ATTN_OUT
    o = 3 * c
    k = w[:, o + na:o + 2 * na]
    v = w[:, o + 2 * na:o + 3 * na]
    kv = []
    for g in range(N_GROUPS):
        kv += [k[:, g * ATTN_OUT:(g + 1) * ATTN_OUT], v[:, g * ATTN_OUT:(g + 1) * ATTN_OUT]]
    return jnp.concatenate([w[:, :o + na]] + kv + [w[:, o + 3 * na:]], axis=1).astype(BF16)


def _tail(attn, gya, sgb, x, wts):
    x1, h2, h2t = _mix(attn, gya, sgb, x, wts["wob"], wts["wo"], wts["g2"])
    s_all, ec_all, tau = _route(h2, wts["wq"], wts["keys"])
    return _experts(h2t, wts["ub"], wts["vtb"], s_all, ec_all, tau, x1, wts["gf"])


def kernel(x_prompt, x_sample, cache_kv_w128, cache_kv_w512, cache_kv_w2048, state_conv, norm1_g, w_in,
           conv_w, w_out_a, w_out_b, w_o, norm2_g, peer_wq, peer_keys, peer_u, peer_v, final_g):
    depth = w_in.shape[0]
    assert depth == 1, "single-layer trunk"
    b, s, d = x_prompt.shape
    nb, nt, _ = x_sample.shape
    c = conv_w.shape[-1]
    nh = peer_keys.shape[2]

    ub, vtb = _prep_tables(peer_u[0], peer_v[0])
    wts = dict(
        wob=w_out_b[0].astype(BF16), wo=w_o[0].astype(BF16), g2=norm2_g[0][None, :],
        wq=peer_wq[0].astype(BF16),
        keys=jnp.swapaxes(peer_keys[0], 0, 1).reshape(2 * nh, PEER_NKEYS, -1).astype(BF16),
        ub=ub, vtb=vtb, gf=final_g[None, :],
    )
    win = _arrange_w_in(w_in[0], c, d)
    woa = w_out_a[0].astype(BF16)
    g1 = norm1_g[0][None, :]
    cw = conv_w[0]

    q, kv0, kv1, kv2, gya, sgb, cst_p = _proj_prompt(x_prompt, g1, win, cw, woa)
    kvs = (kv0, kv1, kv2)
    attn = []
    for g in range(N_GROUPS):
        attn += list(_attn_prompt_group(q, kvs[g], g))
    n = b * s
    y_prompt = _tail(attn, gya.reshape(n, d), sgb.reshape(n, d), x_prompt.reshape(n, d), wts).reshape(b, s, d)
    kv_prompt = []
    for g, (win_g, _) in enumerate(GROUPS):
        keep = min(win_g, s)
        kv_prompt.append(kvs[g][:, s - keep:].reshape(1, b, keep, 2, N_SLOTS, HEAD_DIM))

    x_tm = jnp.swapaxes(x_sample, 0, 1).reshape(nt * nb, d)
    st_tm = jnp.swapaxes(state_conv[0], 0, 1).reshape((CONV_WIDTH - 1) * nb, c)
    qs, ks0, ks1, ks2, gya_s, sgb_s, cst_s = _proj_sample(x_tm, st_tm, g1, win, cw, woa, nb, nt)
    news = [k.reshape(nt, nb, 2 * ATTN_OUT) for k in (ks0, ks1, ks2)]
    caches6 = [cache_kv_w128[0], cache_kv_w512[0], cache_kv_w2048[0]]
    caches = [cv.reshape(nb, cv.shape[1], 2 * ATTN_OUT) for cv in caches6]
    ob_s = _attn_sample(qs.reshape(nt, nb, -1), news, caches, nb, nt).reshape(nt * nb, ATTN_OUT)
    y_s = _tail([ob_s], gya_s, sgb_s, x_tm, wts)
    y_sample = jnp.swapaxes(y_s.reshape(nt, nb, d), 0, 1)
    news6 = [k.reshape(nt, nb, 2, N_SLOTS, HEAD_DIM) for k in news]
    kv_sample = [o[None] for o in _cache_update(caches6, news6, nt)]
    conv_sample = jnp.swapaxes(cst_s.reshape(CONV_WIDTH - 1, nb, c), 0, 1)[None]

    return (y_prompt, y_sample, kv_prompt[0], kv_prompt[1], kv_prompt[2], cst_p[None],
            kv_sample[0], kv_sample[1], kv_sample[2], conv_sample)
```

```python
import functools

import jax
import jax.numpy as jnp
import numpy as np
from jax import lax
from jax.experimental import pallas as pl
from jax.experimental.pallas import tpu as pltpu

F32 = jnp.float32
BF16 = jnp.bfloat16

HEAD_DIM = 64
N_SLOTS = 8
GROUPS = ((128, 1), (512, 4), (2048, 16))
N_GROUPS = len(GROUPS)
ATTN_OUT = N_SLOTS * HEAD_DIM
WIN = 128
CONV_WIDTH = 3
PEER_TOPK = 16
PEER_NKEYS = 128
EPS = 1e-6
NEG = -1e30

LANES = 128
VMEM_LIMIT = 56 * 1024 * 1024

_NT = (((1,), (1,)), ((), ()))


def _params(sem):
    return pltpu.CompilerParams(dimension_semantics=sem, vmem_limit_bytes=VMEM_LIMIT)


def _resident():
    return pl.BlockSpec(memory_space=pltpu.VMEM)


def _rms(x, g):
    return x * lax.rsqrt(jnp.mean(x * x, axis=-1, keepdims=True) + EPS) * g


def _slopes():
    return np.exp2(-8.0 * np.arange(1, N_SLOTS + 1, dtype=np.float64) / N_SLOTS)


def _prep_body(u_ref, v_ref, ub_ref, vt_ref):
    ub_ref[...] = u_ref[...].astype(BF16)
    vt_ref[...] = v_ref[...].T.astype(BF16)


def _prep_tables(u, v):
    e, d = u.shape
    blk = 512
    return pl.pallas_call(
        _prep_body,
        grid=(e // blk,),
        in_specs=[pl.BlockSpec((blk, d), lambda i: (i, 0)), pl.BlockSpec((blk, d), lambda i: (i, 0))],
        out_specs=[pl.BlockSpec((blk, d), lambda i: (i, 0)), pl.BlockSpec((d, blk), lambda i: (0, i))],
        out_shape=[jax.ShapeDtypeStruct((e, d), BF16), jax.ShapeDtypeStruct((d, e), BF16)],
        compiler_params=_params(("arbitrary",)),
        name="prep_tables",
    )(u, v)


def _proj_core(h, win_ref, woa_ref, conv_fn, c, d, q_ref, kv_refs, gya_ref, sgb_ref):
    def proj(a, n):
        return jnp.dot(h, win_ref[:, a:a + n], preferred_element_type=F32)

    o_q = 3 * c
    o_kv = o_q + N_GROUPS * ATTN_OUT
    o_ga = o_kv + N_GROUPS * 2 * ATTN_OUT
    o_gb = o_ga + d
    u = proj(c, c) * proj(2 * c, c)
    y = conv_fn(u)
    ya = jnp.dot((proj(0, c) * y).astype(BF16), woa_ref[...], preferred_element_type=F32)
    gya_ref[...] = jax.nn.sigmoid(proj(o_ga, d)) * ya
    sgb_ref[...] = jax.nn.sigmoid(proj(o_gb, d))
    if q_ref is not None:
        q_ref[...] = proj(o_q, N_GROUPS * ATTN_OUT) * (HEAD_DIM ** -0.5)
        for g in range(N_GROUPS):
            kv_refs[g][...] = proj(o_kv + g * 2 * ATTN_OUT, 2 * ATTN_OUT)


def _proj_prompt_body(x_ref, g1_ref, win_ref, cw_ref, woa_ref,
                      q_ref, kv0_ref, kv1_ref, kv2_ref, gya_ref, sgb_ref, cst_ref, uext_ref, *, tm, c, d):
    j = pl.program_id(1)
    pre = 8

    @pl.when(j == 0)
    def _():
        uext_ref[0:pre, :] = jnp.zeros((pre, c), F32)

    def conv(u):
        uext_ref[pre:pre + tm, :] = u
        cw = cw_ref[...]
        return (cw[0:1] * uext_ref[pre - 2:pre - 2 + tm, :]
                + cw[1:2] * uext_ref[pre - 1:pre - 1 + tm, :]
                + cw[2:3] * u)

    h = _rms(x_ref[...], g1_ref[...]).astype(BF16)
    _proj_core(h, win_ref, woa_ref, conv, c, d, q_ref, (kv0_ref, kv1_ref, kv2_ref), gya_ref, sgb_ref)
    cst_ref[...] = uext_ref[pre + tm - 2:pre + tm, :]
    uext_ref[0:pre, :] = uext_ref[tm:tm + pre, :]


def _proj_prompt(x, g1, win, cw, woa, tm=256):
    b, s, d = x.shape
    c = cw.shape[1]
    nq = N_GROUPS * ATTN_OUT
    tok = lambda n: pl.BlockSpec((None, tm, n), lambda i, j: (i, j, 0))
    outs = [jax.ShapeDtypeStruct((b, s, nq), F32)]
    outs += [jax.ShapeDtypeStruct((b, s, 2 * ATTN_OUT), F32)] * N_GROUPS
    outs += [jax.ShapeDtypeStruct((b, s, d), F32)] * 2
    outs += [jax.ShapeDtypeStruct((b, CONV_WIDTH - 1, c), F32)]
    return pl.pallas_call(
        functools.partial(_proj_prompt_body, tm=tm, c=c, d=d),
        grid=(b, s // tm),
        in_specs=[tok(d), _resident(), _resident(), _resident(), _resident()],
        out_specs=[tok(nq)] + [tok(2 * ATTN_OUT)] * N_GROUPS + [tok(d), tok(d),
                   pl.BlockSpec((None, CONV_WIDTH - 1, c), lambda i, j: (i, 0, 0))],
        out_shape=outs,
        scratch_shapes=[pltpu.VMEM((tm + 8, c), F32)],
        compiler_params=_params(("arbitrary", "arbitrary")),
        name="proj_prompt",
    )(x, g1, win, cw, woa)


def _proj_sample_body(x_ref, xb_ref, st_ref, g1_ref, win_ref, cw_ref, woa_ref,
                      gya_ref, sgb_ref, cst_ref, cols_ref, uext_ref, *, nb, nt, c, d):
    npre = (CONV_WIDTH - 1) * nb
    o_q = 3 * c
    nqkv = 3 * N_GROUPS * ATTN_OUT
    hb = _rms(xb_ref[...], g1_ref[...]).astype(BF16)
    qkv = jnp.dot(hb, win_ref[:, o_q:o_q + nqkv], preferred_element_type=F32)
    nq = N_GROUPS * ATTN_OUT
    cols_ref[0:nq, :] = (qkv[:, :nq] * (HEAD_DIM ** -0.5)).T
    cols_ref[nq:, :] = qkv[:, nq:].T

    def conv(u):
        uext_ref[0:npre, :] = st_ref[...]
        uext_ref[npre:npre + nt * nb, :] = u
        cw = cw_ref[...]
        return (cw[0:1] * uext_ref[0:nt * nb, :]
                + cw[1:2] * uext_ref[nb:nb + nt * nb, :]
                + cw[2:3] * u)

    h = _rms(x_ref[...], g1_ref[...]).astype(BF16)
    _proj_core(h, win_ref, woa_ref, conv, c, d, None, None, gya_ref, sgb_ref)
    cst_ref[...] = uext_ref[nt * nb:nt * nb + npre, :]


def _proj_sample(x_tm, x_bm, st_tm, g1, win, cw, woa, nb, nt):
    n, d = x_tm.shape
    c = cw.shape[1]
    npre = (CONV_WIDTH - 1) * nb
    outs = [jax.ShapeDtypeStruct((n, d), F32)] * 2
    outs += [jax.ShapeDtypeStruct((npre, c), F32)]
    outs += [jax.ShapeDtypeStruct((3 * N_GROUPS * ATTN_OUT, n), F32)]
    return pl.pallas_call(
        functools.partial(_proj_sample_body, nb=nb, nt=nt, c=c, d=d),
        in_specs=[_resident()] * 7,
        out_specs=[_resident()] * 4,
        out_shape=outs,
        scratch_shapes=[pltpu.VMEM((npre + n, c), F32)],
        compiler_params=pltpu.CompilerParams(vmem_limit_bytes=VMEM_LIMIT),
        name="proj_sample",
    )(x_tm, x_bm, st_tm, g1, win, cw, woa)


def _kvt_body(kv_ref, out_ref):
    out_ref[...] = kv_ref[...].T


def _kv_transposed(kv, keep, tm=256):
    b, s, w = kv.shape
    tm = min(tm, keep)
    off = (s - keep) // tm
    return pl.pallas_call(
        _kvt_body,
        grid=(b, keep // tm),
        in_specs=[pl.BlockSpec((None, tm, w), lambda i, j: (i, j + off, 0))],
        out_specs=pl.BlockSpec((None, w, tm), lambda i, j: (i, 0, j)),
        out_shape=jax.ShapeDtypeStruct((b, w, keep), F32),
        compiler_params=_params(("arbitrary", "arbitrary")),
        name="kv_transposed",
    )(kv)


def _prompt_bias(dil):
    tq = np.arange(WIN)[:, None]
    col = np.arange(2 * WIN)[None, :]
    dist = tq + WIN - col
    ok = (dist >= 0) & (dist <= WIN)
    b = -_slopes()[:, None, None] * (dil * dist)[None]
    return jnp.asarray(np.where(ok[None], b, NEG), F32)


def _attn_prompt_body(q_ref, kvc_ref, kvp_ref, bias_ref, acc_ref, m_ref, l_ref):
    j = pl.program_id(2)
    lo = lax.broadcasted_iota(jnp.int32, (WIN, LANES), 1) < HEAD_DIM
    col = lax.broadcasted_iota(jnp.int32, (WIN, 2 * WIN), 1)
    kill = jnp.logical_and(col < WIN, j == 0)
    for p in range(ATTN_OUT // LANES):
        ls = slice(p * LANES, (p + 1) * LANES)
        vs = slice(ATTN_OUT + p * LANES, ATTN_OUT + (p + 1) * LANES)
        qp = q_ref[:, ls]
        kcat = jnp.concatenate([kvp_ref[:, ls], kvc_ref[:, ls]], axis=0).astype(BF16)
        vcat = jnp.concatenate([kvp_ref[:, vs], kvc_ref[:, vs]], axis=0).astype(BF16)
        res = []
        for hh in range(2):
            qm = jnp.where(lo if hh == 0 else jnp.logical_not(lo), qp, 0.0).astype(BF16)
            s = lax.dot_general(qm, kcat, _NT, preferred_element_type=F32)
            s = jnp.where(kill, NEG, s + bias_ref[2 * p + hh])
            m = jnp.max(s, axis=-1, keepdims=True)
            e = jnp.exp(s - m)
            l = jnp.sum(e, axis=-1, keepdims=True)
            pv = jnp.dot(e.astype(BF16), vcat, preferred_element_type=F32)
            res.append((m, l, pv))
        acc_ref[:, ls] = jnp.where(lo, res[0][2], res[1][2])
        m_ref[:, ls] = jnp.where(lo, res[0][0], res[1][0])
        l_ref[:, ls] = jnp.where(lo, res[0][1], res[1][1])


def _attn_prompt_group(q, kv, g):
    b, s, nq = q.shape
    dil = GROUPS[g][1]
    sr = s // dil
    nblk = sr // WIN
    qv = q.reshape(b, sr, dil * nq)
    kvv = kv.reshape(b, sr, dil * 2 * ATTN_OUT)
    nqb = nq // ATTN_OUT
    out_sd = jax.ShapeDtypeStruct((b, sr, dil * ATTN_OUT), F32)
    o_spec = pl.BlockSpec((None, WIN, ATTN_OUT), lambda i, r, j: (i, j, r))
    acc, m, l = pl.pallas_call(
        _attn_prompt_body,
        grid=(b, dil, nblk),
        in_specs=[
            pl.BlockSpec((None, WIN, ATTN_OUT), lambda i, r, j: (i, j, r * nqb + g)),
            pl.BlockSpec((None, WIN, 2 * ATTN_OUT), lambda i, r, j: (i, j, r)),
            pl.BlockSpec((None, WIN, 2 * ATTN_OUT), lambda i, r, j: (i, jnp.maximum(j - 1, 0), r)),
            _resident(),
        ],
        out_specs=[o_spec, o_spec, o_spec],
        out_shape=[out_sd, out_sd, out_sd],
        compiler_params=_params(("arbitrary", "arbitrary", "arbitrary")),
        name=f"attn_prompt_g{g}",
    )(qv, kvv, kvv, _prompt_bias(dil))
    shp = (b * s, ATTN_OUT)
    return acc.reshape(shp), m.reshape(shp), l.reshape(shp)


def _sample_bias(nt):
    sl = _slopes()[None, :, None]
    cache, new = [], []
    for win, dil in GROUPS:
        t = np.arange(nt)[:, None, None]
        dist = win + t - np.arange(win)[None, None, :]
        ok = (dist % dil == 0) & (dist <= win)
        cache.append(jnp.asarray(np.where(ok, -sl * dist, NEG)[:, :, None, :], F32))
        dn = t - np.arange(LANES)[None, None, :]
        okn = (dn >= 0) & (dn % dil == 0)
        new.append(np.where(okn, -sl * dn, NEG)[:, :, None, :])
    return cache, jnp.asarray(np.stack(new), F32)


def _sample_body(cols_ref, c0_ref, c1_ref, c2_ref, b0_ref, b1_ref, b2_ref, bn_ref,
                 o0_ref, o1_ref, o2_ref, att_ref, *, nt, hc):
    b = pl.program_id(0)
    col0 = b * nt
    tile = pl.ds(pl.multiple_of((col0 // LANES) * LANES, LANES), LANES)
    to_lane0 = (LANES - col0 % LANES) % LANES
    lane = lax.broadcasted_iota(jnp.int32, (HEAD_DIM, LANES), 1)
    keep = lane < LANES - nt
    c_refs = (c0_ref, c1_ref, c2_ref)
    o_refs = (o0_ref, o1_ref, o2_ref)
    bias_refs = (b0_ref, b1_ref, b2_ref)
    nq = N_GROUPS * ATTN_OUT

    def per_head(hl, carry):
        h = pl.program_id(1) * hc + hl
        stats = [[] for _ in range(nt)]
        for g in range(N_GROUPS):
            wb = c_refs[g].shape[-1]
            ntile = wb // LANES

            def cols(row0):
                rows = pl.ds(pl.multiple_of(row0 + h * HEAD_DIM, HEAD_DIM), HEAD_DIM)
                return pltpu.roll(cols_ref[rows, tile], to_lane0, axis=1)

            q_t = cols(g * ATTN_OUT)
            new = (cols(nq + g * 2 * ATTN_OUT), cols(nq + g * 2 * ATTN_OUT + ATTN_OUT))
            old = (c_refs[g][0, hl], c_refs[g][1, hl])
            for kv in range(2):
                rot = [pltpu.roll(old[kv][:, j * LANES:(j + 1) * LANES], LANES - nt, axis=1) for j in range(ntile)]
                rot.append(pltpu.roll(new[kv], LANES - nt, axis=1))
                for j in range(ntile):
                    o_refs[g][kv, hl, :, j * LANES:(j + 1) * LANES] = jnp.where(keep, rot[j], rot[j + 1])
            for t in range(nt):
                qc = q_t[:, t:t + 1]
                s = jnp.sum(old[0] * qc, axis=0, keepdims=True) + bias_refs[g][t, h]
                sn = jnp.sum(new[0] * qc, axis=0, keepdims=True) + bn_ref[g, t, h]
                m = jnp.maximum(jnp.max(s, axis=1, keepdims=True), jnp.max(sn, axis=1, keepdims=True))
                e = jnp.exp(s - m)
                en = jnp.exp(sn - m)
                l = jnp.sum(e, axis=1, keepdims=True) + jnp.sum(en, axis=1, keepdims=True)
                acc = (jnp.sum(old[1] * e, axis=1, keepdims=True)
                       + jnp.sum(new[1] * en, axis=1, keepdims=True))
                stats[t].append((m, l, acc))
        out = jnp.zeros((HEAD_DIM, LANES), F32)
        for t in range(nt):
            mm = functools.reduce(jnp.maximum, [st[0] for st in stats[t]])
            num = 0.0
            den = 0.0
            for m, l, acc in stats[t]:
                w = jnp.exp(m - mm)
                num = num + w * acc
                den = den + w * l
            out = jnp.where(lane == t, num / den, out)
        att_ref[pl.ds(pl.multiple_of(h * HEAD_DIM, HEAD_DIM), HEAD_DIM), :] = out
        return carry

    lax.fori_loop(0, hc, per_head, 0)


def _sample_attention(cols, caches_t, nb, nt, hc=4):
    cbias, nbias = _sample_bias(nt)
    for g, (win, _) in enumerate(GROUPS):
        assert caches_t[g].shape[-1] == win, "window buffer must hold a full window"
    assert nt <= LANES and LANES % nt == 0 and (nb * nt) % LANES == 0
    cspec = lambda wb: pl.BlockSpec((None, 2, hc, HEAD_DIM, wb), lambda i, j: (i, 0, j, 0, 0))
    cspecs = [cspec(c.shape[-1]) for c in caches_t]
    res = pl.pallas_call(
        functools.partial(_sample_body, nt=nt, hc=hc),
        grid=(nb, N_SLOTS // hc),
        in_specs=[_resident()] + cspecs + [_resident()] * (N_GROUPS + 1),
        out_specs=cspecs + [pl.BlockSpec((None, ATTN_OUT, LANES), lambda i, j: (i, 0, 0))],
        out_shape=[jax.ShapeDtypeStruct(c.shape, F32) for c in caches_t]
                  + [jax.ShapeDtypeStruct((nb, ATTN_OUT, LANES), F32)],
        compiler_params=_params(("arbitrary", "arbitrary")),
        name="sample_attention",
    )(cols, *caches_t, *cbias, nbias)
    return res[:N_GROUPS], res[N_GROUPS]


def _mix_body(*refs, combine):
    if combine:
        stats = [refs[3 * g:3 * g + 3] for g in range(N_GROUPS)]
        rest = refs[3 * N_GROUPS:]
        mm = functools.reduce(jnp.maximum, [st[1][...] for st in stats])
        num = 0.0
        den = 0.0
        for a_ref, m_ref, l_ref in stats:
            w = jnp.exp(m_ref[...] - mm)
            num = num + w * a_ref[...]
            den = den + w * l_ref[...]
        ob = num / den
    else:
        ob = refs[0][...]
        rest = refs[1:]
    gya_ref, sgb_ref, x_ref, wob_ref, wo_ref, g2_ref, x1_ref, h2_ref, h2t_ref = rest
    yb = jnp.dot(ob.astype(BF16), wob_ref[...], preferred_element_type=F32)
    mix = gya_ref[...] + sgb_ref[...] * yb
    x1 = x_ref[...] + jnp.dot(mix.astype(BF16), wo_ref[...], preferred_element_type=F32)
    x1_ref[...] = x1
    h2 = _rms(x1, g2_ref[...])
    h2_ref[...] = h2.astype(BF16)
    h2t_ref[...] = h2.T.astype(BF16)


def _mix(attn, gya, sgb, x, wob, wo, g2, tm=512):
    n, d = x.shape
    tm = min(tm, n)
    combine = len(attn) > 1
    tok = lambda w: pl.BlockSpec((tm, w), lambda i: (i, 0))
    return pl.pallas_call(
        functools.partial(_mix_body, combine=combine),
        grid=(n // tm,),
        in_specs=[tok(ATTN_OUT)] * len(attn) + [tok(d), tok(d), tok(d), _resident(), _resident(), _resident()],
        out_specs=[tok(d), tok(d), pl.BlockSpec((d, tm), lambda i: (0, i))],
        out_shape=[jax.ShapeDtypeStruct((n, d), F32), jax.ShapeDtypeStruct((n, d), BF16),
                   jax.ShapeDtypeStruct((d, n), BF16)],
        compiler_params=_params(("arbitrary",)),
        name="mix_combine" if combine else "mix",
    )(*attn, gya, sgb, x, wob, wo, g2)


def _extract_top(work, k):
    nrow = work.shape[0]
    iota = lax.broadcasted_iota(jnp.int32, work.shape, 0)
    vals = []
    for _ in range(k):
        mx = jnp.max(work, axis=0, keepdims=True)
        vals.append(mx)
        first = jnp.min(jnp.where(work == mx, iota, nrow), axis=0, keepdims=True)
        work = jnp.where(iota == first, -jnp.inf, work)
    return vals


def _route_body(h2_ref, wq_ref, keys_ref, s_ref, ec_ref, tau_ref, vals_ref, *, tm, nh):
    q = jnp.dot(h2_ref[...], wq_ref[...], preferred_element_type=F32).astype(BF16)
    for hh in range(2 * nh):
        s_ref[hh] = lax.dot_general(keys_ref[hh], q[:, hh * PEER_NKEYS:(hh + 1) * PEER_NKEYS], _NT,
                                    preferred_element_type=F32)
    nchunk = tm // LANES

    def top_keys(i, carry):
        hh = i // nchunk
        cs = pl.ds(pl.multiple_of((i % nchunk) * LANES, LANES), LANES)
        vals = _extract_top(s_ref[hh, :, cs], PEER_TOPK)
        for k, v in enumerate(vals):
            vals_ref[hh, k:k + 1, cs] = v
        return carry

    lax.fori_loop(0, 2 * nh * nchunk, top_keys, 0)

    r8 = lax.broadcasted_iota(jnp.int32, (8, LANES), 0)

    def top_pairs(i, carry):
        h = i // nchunk
        cs = pl.ds(pl.multiple_of((i % nchunk) * LANES, LANES), LANES)
        v1 = vals_ref[2 * h, :, cs]
        v2 = vals_ref[2 * h + 1, :, cs]
        blocks = [v1[0:1] + v2]
        for a in range(1, 8):
            nb_ = PEER_TOPK // (a + 1)
            blk = v1[a:a + 1] + v2[0:8]
            blocks.append(blk if nb_ >= 8 else jnp.where(r8 < nb_, blk, -jnp.inf))
        blocks.append(v1[8:16] + v2[0:1])
        sv = _extract_top(jnp.concatenate(blocks, axis=0), PEER_TOPK)
        z = functools.reduce(lambda a, b: a + b, [jnp.exp(v - sv[0]) for v in sv])
        tau_ref[h, :, cs] = sv[PEER_TOPK - 1]
        ec_ref[2 * h, :, cs] = jnp.exp(s_ref[2 * h, :, cs] - v1[0:1]) / z
        ec_ref[2 * h + 1, :, cs] = jnp.exp(s_ref[2 * h + 1, :, cs] - v2[0:1])
        return carry

    lax.fori_loop(0, nh * nchunk, top_pairs, 0)


def _route(h2, wq, keys, tm=512):
    n, d = h2.shape
    nh2 = keys.shape[0]
    tm = min(tm, n)
    big = pl.BlockSpec((nh2, PEER_NKEYS, tm), lambda i: (0, 0, i))
    return pl.pallas_call(
        functools.partial(_route_body, tm=tm, nh=nh2 // 2),
        grid=(n // tm,),
        in_specs=[pl.BlockSpec((tm, d), lambda i: (i, 0)), _resident(), _resident()],
        out_specs=[big, big, pl.BlockSpec((nh2 // 2, 1, tm), lambda i: (0, 0, i))],
        out_shape=[jax.ShapeDtypeStruct((nh2, PEER_NKEYS, n), F32), jax.ShapeDtypeStruct((nh2, PEER_NKEYS, n), F32),
                   jax.ShapeDtypeStruct((nh2 // 2, 1, n), F32)],
        scratch_shapes=[pltpu.VMEM((nh2, PEER_TOPK, tm), F32)],
        compiler_params=_params(("arbitrary",)),
        name="peer_route",
    )(h2, wq, keys)


def _experts_body(h2t_ref, u_ref, vt_ref, s_ref, ec_ref, tau_ref, x1_ref, gf_ref, y_ref,
                  acc_ref, act_ref, p_ref, *, tm, eblk, nh):
    j = pl.program_id(1)
    nchunk = tm // LANES
    nsub = eblk // PEER_NKEYS

    @pl.when(j == 0)
    def _():
        acc_ref[...] = jnp.zeros_like(acc_ref)

    act_ref[...] = jnp.dot(u_ref[...], h2t_ref[...], preferred_element_type=F32)
    r8 = lax.broadcasted_iota(jnp.int32, (8, LANES), 0)

    def gate(i, carry):
        sub = i // nchunk
        cs = pl.ds(pl.multiple_of((i % nchunk) * LANES, LANES), LANES)
        rs = pl.ds(pl.multiple_of(sub * PEER_NKEYS, PEER_NKEYS), PEER_NKEYS)
        i1 = j * nsub + sub
        grp = pl.ds(pl.multiple_of((i1 // 8) * 8, 8), 8)
        pick = r8 == (i1 % 8)
        g = jnp.zeros((PEER_NKEYS, LANES), F32)
        for h in range(nh):
            s1 = jnp.sum(jnp.where(pick, s_ref[2 * h, grp, cs], 0.0), axis=0, keepdims=True)
            c1 = jnp.sum(jnp.where(pick, ec_ref[2 * h, grp, cs], 0.0), axis=0, keepdims=True)
            pair = s_ref[2 * h + 1, :, cs] + s1
            w = ec_ref[2 * h + 1, :, cs] * c1
            g = g + jnp.where(pair >= tau_ref[h, :, cs], w, 0.0)
        a = act_ref[rs, cs]
        gelu = 0.5 * a * (1.0 + lax.erf(a * np.float32(np.sqrt(0.5))))
        p_ref[rs, cs] = (g * gelu).astype(BF16)
        return carry

    lax.fori_loop(0, nsub * nchunk, gate, 0)
    acc_ref[...] += jnp.dot(vt_ref[...], p_ref[...], preferred_element_type=F32)

    @pl.when(j == pl.num_programs(1) - 1)
    def _():
        y_ref[...] = _rms(x1_ref[...] + acc_ref[...].T, gf_ref[...])


def _experts(h2t, ub, vtb, s_all, ec_all, tau, x1, gf, tm=512, eblk=512):
    d, n = h2t.shape
    e = ub.shape[0]
    nh2 = s_all.shape[0]
    tm = min(tm, n)
    big = pl.BlockSpec((nh2, PEER_NKEYS, tm), lambda i, j: (0, 0, i))
    return pl.pallas_call(
        functools.partial(_experts_body, tm=tm, eblk=eblk, nh=nh2 // 2),
        grid=(n // tm, e // eblk),
        in_specs=[
            pl.BlockSpec((d, tm), lambda i, j: (0, i)),
            pl.BlockSpec((eblk, d), lambda i, j: (j, 0)),
            pl.BlockSpec((d, eblk), lambda i, j: (0, j)),
            big, big,
            pl.BlockSpec((nh2 // 2, 1, tm), lambda i, j: (0, 0, i)),
            pl.BlockSpec((tm, d), lambda i, j: (i, 0)),
            _resident(),
        ],
        out_specs=pl.BlockSpec((tm, d), lambda i, j: (i, 0)),
        out_shape=jax.ShapeDtypeStruct((n, d), F32),
        scratch_shapes=[pltpu.VMEM((d, tm), F32), pltpu.VMEM((eblk, tm), F32), pltpu.VMEM((eblk, tm), BF16)],
        compiler_params=_params(("arbitrary", "arbitrary")),
        name="peer_experts",
    )(h2t, ub, vtb, s_all, ec_all, tau, x1, gf)


def _arrange_w_in(w, c, d):
    na = N_GROUPS * ATTN_OUT
    o = 3 * c
    k = w[:, o + na:o + 2 * na]
    v = w[:, o + 2 * na:o + 3 * na]
    kv = []
    for g in range(N_GROUPS):
        kv += [k[:, g * ATTN_OUT:(g + 1) * ATTN_OUT], v[:, g * ATTN_OUT:(g + 1) * ATTN_OUT]]
    return jnp.concatenate([w[:, :o + na]] + kv + [w[:, o + 3 * na:]], axis=1).astype(BF16)


def _tail(attn, gya, sgb, x, wts):
    x1, h2, h2t = _mix(attn, gya, sgb, x, wts["wob"], wts["wo"], wts["g2"])
    s_all, ec_all, tau = _route(h2, wts["wq"], wts["keys"])
    return _experts(h2t, wts["ub"], wts["vtb"], s_all, ec_all, tau, x1, wts["gf"])


def kernel(x_prompt, x_sample, cache_kv_w128, cache_kv_w512, cache_kv_w2048, state_conv, norm1_g, w_in,
           conv_w, w_out_a, w_out_b, w_o, norm2_g, peer_wq, peer_keys, peer_u, peer_v, final_g):
    depth = w_in.shape[0]
    assert depth == 1, "single-layer trunk"
    b, s, d = x_prompt.shape
    nb, nt, _ = x_sample.shape
    c = conv_w.shape[-1]
    nh = peer_keys.shape[2]

    ub, vtb = _prep_tables(peer_u[0], peer_v[0])
    wts = dict(
        wob=w_out_b[0].astype(BF16), wo=w_o[0].astype(BF16), g2=norm2_g[0][None, :],
        wq=peer_wq[0].astype(BF16),
        keys=jnp.swapaxes(peer_keys[0], 0, 1).reshape(2 * nh, PEER_NKEYS, -1).astype(BF16),
        ub=ub, vtb=vtb, gf=final_g[None, :],
    )
    win = _arrange_w_in(w_in[0], c, d)
    woa = w_out_a[0].astype(BF16)
    g1 = norm1_g[0][None, :]
    cw = conv_w[0]

    q, kv0, kv1, kv2, gya, sgb, cst_p = _proj_prompt(x_prompt, g1, win, cw, woa)
    kvs = (kv0, kv1, kv2)
    attn = []
    for g in range(N_GROUPS):
        attn += list(_attn_prompt_group(q, kvs[g], g))
    n = b * s
    y_prompt = _tail(attn, gya.reshape(n, d), sgb.reshape(n, d), x_prompt.reshape(n, d), wts).reshape(b, s, d)
    to_rows_major = lambda a: jnp.transpose(a, (0, 4, 1, 2, 3))[None]
    kv_prompt = []
    for g, (win_g, _) in enumerate(GROUPS):
        keep = min(win_g, s)
        kt = _kv_transposed(kvs[g], keep)
        kv_prompt.append(to_rows_major(kt.reshape(b, 2, N_SLOTS, HEAD_DIM, keep)))

    x_tm = jnp.swapaxes(x_sample, 0, 1).reshape(nt * nb, d)
    st_tm = jnp.swapaxes(state_conv[0], 0, 1).reshape((CONV_WIDTH - 1) * nb, c)
    gya_s, sgb_s, cst_s, cols = _proj_sample(x_tm, x_sample.reshape(nb * nt, d), st_tm, g1, win, cw, woa, nb, nt)
    caches_t = [jnp.transpose(cv[0], (0, 2, 3, 4, 1)) for cv in (cache_kv_w128, cache_kv_w512, cache_kv_w2048)]
    new_caches, att = _sample_attention(cols, caches_t, nb, nt)
    ob_s = jnp.transpose(att[:, :, :nt], (2, 0, 1)).reshape(nt * nb, ATTN_OUT)
    y_s = _tail([ob_s], gya_s, sgb_s, x_tm, wts)
    y_sample = jnp.swapaxes(y_s.reshape(nt, nb, d), 0, 1)
    kv_sample = [to_rows_major(o) for o in new_caches]
    conv_sample = jnp.swapaxes(cst_s.reshape(CONV_WIDTH - 1, nb, c), 0, 1)[None]

    return (y_prompt, y_sample, kv_prompt[0], kv_prompt[1], kv_prompt[2], cst_p[None],
            kv_sample[0], kv_sample[1], kv_sample[2], conv_sample)
```

```python
import functools

import jax
import jax.numpy as jnp
import numpy as np
from jax import lax
from jax.experimental import pallas as pl
from jax.experimental.pallas import tpu as pltpu

F32 = jnp.float32
BF16 = jnp.bfloat16

HEAD_DIM = 64
N_SLOTS = 8
GROUPS = ((128, 1), (512, 4), (2048, 16))
N_GROUPS = len(GROUPS)
ATTN_OUT = N_SLOTS * HEAD_DIM
WIN = 128
CONV_WIDTH = 3
PEER_TOPK = 16
PEER_NKEYS = 128
EPS = 1e-6
NEG = -1e30

LANES = 128
VMEM_LIMIT = 56 * 1024 * 1024

_NT = (((1,), (1,)), ((), ()))


def _params(sem):
    return pltpu.CompilerParams(dimension_semantics=sem, vmem_limit_bytes=VMEM_LIMIT)


def _resident():
    return pl.BlockSpec(memory_space=pltpu.VMEM)


def _rms(x, g):
    return x * lax.rsqrt(jnp.mean(x * x, axis=-1, keepdims=True) + EPS) * g


def _slopes():
    return np.exp2(-8.0 * np.arange(1, N_SLOTS + 1, dtype=np.float64) / N_SLOTS)


def _prep_body(u_ref, v_ref, ub_ref, vt_ref):
    ub_ref[...] = u_ref[...].astype(BF16)
    vt_ref[...] = v_ref[...].T.astype(BF16)


def _prep_tables(u, v):
    e, d = u.shape
    blk = 512
    return pl.pallas_call(
        _prep_body,
        grid=(e // blk,),
        in_specs=[pl.BlockSpec((blk, d), lambda i: (i, 0)), pl.BlockSpec((blk, d), lambda i: (i, 0))],
        out_specs=[pl.BlockSpec((blk, d), lambda i: (i, 0)), pl.BlockSpec((d, blk), lambda i: (0, i))],
        out_shape=[jax.ShapeDtypeStruct((e, d), BF16), jax.ShapeDtypeStruct((d, e), BF16)],
        compiler_params=_params(("arbitrary",)),
        name="prep_tables",
    )(u, v)


def _proj_core(h, win_ref, woa_ref, conv_fn, c, d, q_ref, kv_refs, gya_ref, sgb_ref):
    def proj(a, n):
        return jnp.dot(h, win_ref[:, a:a + n], preferred_element_type=F32)

    o_q = 3 * c
    o_kv = o_q + N_GROUPS * ATTN_OUT
    o_ga = o_kv + N_GROUPS * 2 * ATTN_OUT
    o_gb = o_ga + d
    u = proj(c, c) * proj(2 * c, c)
    y = conv_fn(u)
    ya = jnp.dot((proj(0, c) * y).astype(BF16), woa_ref[...], preferred_element_type=F32)
    gya_ref[...] = jax.nn.sigmoid(proj(o_ga, d)) * ya
    sgb_ref[...] = jax.nn.sigmoid(proj(o_gb, d))
    if q_ref is not None:
        q_ref[...] = proj(o_q, N_GROUPS * ATTN_OUT) * (HEAD_DIM ** -0.5)
        for g in range(N_GROUPS):
            kv_refs[g][...] = proj(o_kv + g * 2 * ATTN_OUT, 2 * ATTN_OUT)


def _proj_prompt_body(x_ref, g1_ref, win_ref, cw_ref, woa_ref,
                      q_ref, kv0_ref, kv1_ref, kv2_ref, gya_ref, sgb_ref, cst_ref, uext_ref, *, tm, c, d):
    j = pl.program_id(1)
    pre = 8

    @pl.when(j == 0)
    def _():
        uext_ref[0:pre, :] = jnp.zeros((pre, c), F32)

    def conv(u):
        uext_ref[pre:pre + tm, :] = u
        cw = cw_ref[...]
        return (cw[0:1] * uext_ref[pre - 2:pre - 2 + tm, :]
                + cw[1:2] * uext_ref[pre - 1:pre - 1 + tm, :]
                + cw[2:3] * u)

    h = _rms(x_ref[...], g1_ref[...]).astype(BF16)
    _proj_core(h, win_ref, woa_ref, conv, c, d, q_ref, (kv0_ref, kv1_ref, kv2_ref), gya_ref, sgb_ref)
    cst_ref[...] = uext_ref[pre + tm - 2:pre + tm, :]
    uext_ref[0:pre, :] = uext_ref[tm:tm + pre, :]


def _proj_prompt(x, g1, win, cw, woa, tm=256):
    b, s, d = x.shape
    c = cw.shape[1]
    nq = N_GROUPS * ATTN_OUT
    tok = lambda n: pl.BlockSpec((None, tm, n), lambda i, j: (i, j, 0))
    outs = [jax.ShapeDtypeStruct((b, s, nq), F32)]
    outs += [jax.ShapeDtypeStruct((b, s, 2 * ATTN_OUT), F32)] * N_GROUPS
    outs += [jax.ShapeDtypeStruct((b, s, d), F32)] * 2
    outs += [jax.ShapeDtypeStruct((b, CONV_WIDTH - 1, c), F32)]
    return pl.pallas_call(
        functools.partial(_proj_prompt_body, tm=tm, c=c, d=d),
        grid=(b, s // tm),
        in_specs=[tok(d), _resident(), _resident(), _resident(), _resident()],
        out_specs=[tok(nq)] + [tok(2 * ATTN_OUT)] * N_GROUPS + [tok(d), tok(d),
                   pl.BlockSpec((None, CONV_WIDTH - 1, c), lambda i, j: (i, 0, 0))],
        out_shape=outs,
        scratch_shapes=[pltpu.VMEM((tm + 8, c), F32)],
        compiler_params=_params(("arbitrary", "arbitrary")),
        name="proj_prompt",
    )(x, g1, win, cw, woa)


def _proj_sample_body(x_ref, xb_ref, st_ref, g1_ref, win_ref, cw_ref, woa_ref,
                      gya_ref, sgb_ref, cst_ref, cols_ref, uext_ref, *, nb, nt, c, d):
    npre = (CONV_WIDTH - 1) * nb
    o_q = 3 * c
    nqkv = 3 * N_GROUPS * ATTN_OUT
    hb = _rms(xb_ref[...], g1_ref[...]).astype(BF16)
    qkv = jnp.dot(hb, win_ref[:, o_q:o_q + nqkv], preferred_element_type=F32)
    nq = N_GROUPS * ATTN_OUT
    cols_ref[0:nq, :] = (qkv[:, :nq] * (HEAD_DIM ** -0.5)).T
    cols_ref[nq:, :] = qkv[:, nq:].T

    def conv(u):
        uext_ref[0:npre, :] = st_ref[...]
        uext_ref[npre:npre + nt * nb, :] = u
        cw = cw_ref[...]
        return (cw[0:1] * uext_ref[0:nt * nb, :]
                + cw[1:2] * uext_ref[nb:nb + nt * nb, :]
                + cw[2:3] * u)

    h = _rms(x_ref[...], g1_ref[...]).astype(BF16)
    _proj_core(h, win_ref, woa_ref, conv, c, d, None, None, gya_ref, sgb_ref)
    cst_ref[...] = uext_ref[nt * nb:nt * nb + npre, :]


def _proj_sample(x_tm, x_bm, st_tm, g1, win, cw, woa, nb, nt):
    n, d = x_tm.shape
    c = cw.shape[1]
    npre = (CONV_WIDTH - 1) * nb
    outs = [jax.ShapeDtypeStruct((n, d), F32)] * 2
    outs += [jax.ShapeDtypeStruct((npre, c), F32)]
    outs += [jax.ShapeDtypeStruct((3 * N_GROUPS * ATTN_OUT, n), F32)]
    return pl.pallas_call(
        functools.partial(_proj_sample_body, nb=nb, nt=nt, c=c, d=d),
        in_specs=[_resident()] * 7,
        out_specs=[_resident()] * 4,
        out_shape=outs,
        scratch_shapes=[pltpu.VMEM((npre + n, c), F32)],
        compiler_params=pltpu.CompilerParams(vmem_limit_bytes=VMEM_LIMIT),
        name="proj_sample",
    )(x_tm, x_bm, st_tm, g1, win, cw, woa)


def _kvt_body(kv_ref, out_ref):
    out_ref[...] = kv_ref[...].T


def _kv_transposed(kv, keep, tm=256):
    b, s, w = kv.shape
    tm = min(tm, keep)
    off = (s - keep) // tm
    return pl.pallas_call(
        _kvt_body,
        grid=(b, keep // tm),
        in_specs=[pl.BlockSpec((None, tm, w), lambda i, j: (i, j + off, 0))],
        out_specs=pl.BlockSpec((None, w, tm), lambda i, j: (i, 0, j)),
        out_shape=jax.ShapeDtypeStruct((b, w, keep), F32),
        compiler_params=_params(("arbitrary", "arbitrary")),
        name="kv_transposed",
    )(kv)


def _prompt_bias(dil):
    tq = np.arange(WIN)[:, None]
    col = np.arange(2 * WIN)[None, :]
    dist = tq + WIN - col
    ok = (dist >= 0) & (dist <= WIN)
    b = -_slopes()[:, None, None] * (dil * dist)[None]
    return jnp.asarray(np.where(ok[None], b, NEG), F32)


def _attn_prompt_body(q_ref, kvc_ref, kvp_ref, bias_ref, acc_ref, m_ref, l_ref):
    j = pl.program_id(2)
    lo = lax.broadcasted_iota(jnp.int32, (WIN, LANES), 1) < HEAD_DIM
    col = lax.broadcasted_iota(jnp.int32, (WIN, 2 * WIN), 1)
    kill = jnp.logical_and(col < WIN, j == 0)
    for p in range(ATTN_OUT // LANES):
        ls = slice(p * LANES, (p + 1) * LANES)
        vs = slice(ATTN_OUT + p * LANES, ATTN_OUT + (p + 1) * LANES)
        qp = q_ref[:, ls]
        kcat = jnp.concatenate([kvp_ref[:, ls], kvc_ref[:, ls]], axis=0).astype(BF16)
        vcat = jnp.concatenate([kvp_ref[:, vs], kvc_ref[:, vs]], axis=0).astype(BF16)
        res = []
        for hh in range(2):
            qm = jnp.where(lo if hh == 0 else jnp.logical_not(lo), qp, 0.0).astype(BF16)
            s = lax.dot_general(qm, kcat, _NT, preferred_element_type=F32)
            s = jnp.where(kill, NEG, s + bias_ref[2 * p + hh])
            m = jnp.max(s, axis=-1, keepdims=True)
            e = jnp.exp(s - m)
            l = jnp.sum(e, axis=-1, keepdims=True)
            pv = jnp.dot(e.astype(BF16), vcat, preferred_element_type=F32)
            res.append((m, l, pv))
        acc_ref[:, ls] = jnp.where(lo, res[0][2], res[1][2])
        m_ref[:, ls] = jnp.where(lo, res[0][0], res[1][0])
        l_ref[:, ls] = jnp.where(lo, res[0][1], res[1][1])


def _attn_prompt_group(q, kv, g):
    b, s, nq = q.shape
    dil = GROUPS[g][1]
    sr = s // dil
    nblk = sr // WIN
    qv = q.reshape(b, sr, dil * nq)
    kvv = kv.reshape(b, sr, dil * 2 * ATTN_OUT)
    nqb = nq // ATTN_OUT
    out_sd = jax.ShapeDtypeStruct((b, sr, dil * ATTN_OUT), F32)
    o_spec = pl.BlockSpec((None, WIN, ATTN_OUT), lambda i, r, j: (i, j, r))
    acc, m, l = pl.pallas_call(
        _attn_prompt_body,
        grid=(b, dil, nblk),
        in_specs=[
            pl.BlockSpec((None, WIN, ATTN_OUT), lambda i, r, j: (i, j, r * nqb + g)),
            pl.BlockSpec((None, WIN, 2 * ATTN_OUT), lambda i, r, j: (i, j, r)),
            pl.BlockSpec((None, WIN, 2 * ATTN_OUT), lambda i, r, j: (i, jnp.maximum(j - 1, 0), r)),
            _resident(),
        ],
        out_specs=[o_spec, o_spec, o_spec],
        out_shape=[out_sd, out_sd, out_sd],
        compiler_params=_params(("arbitrary", "arbitrary", "arbitrary")),
        name=f"attn_prompt_g{g}",
    )(qv, kvv, kvv, _prompt_bias(dil))
    shp = (b * s, ATTN_OUT)
    return acc.reshape(shp), m.reshape(shp), l.reshape(shp)


def _sample_bias(nt):
    sl = _slopes()[:, None, None]
    cache, new = [], []
    for win, dil in GROUPS:
        t = np.arange(8)[None, :, None]
        dist = win + t - np.arange(win)[None, None, :]
        ok = (dist % dil == 0) & (dist <= win)
        cache.append(jnp.asarray(np.where(t < nt, np.where(ok, -sl * dist, NEG), 0.0), F32))
        dn = t - np.arange(LANES)[None, None, :]
        okn = (dn >= 0) & (dn % dil == 0)
        new.append(np.where(t < nt, np.where(okn, -sl * dn, NEG), 0.0))
    return cache, jnp.asarray(np.stack(new), F32)


def _sample_body(cols_ref, c0_ref, c1_ref, c2_ref, b0_ref, b1_ref, b2_ref, bn_ref,
                 o0_ref, o1_ref, o2_ref, att_ref, *, nt, hc):
    b = pl.program_id(0)
    col0 = b * nt
    tile = pl.ds(pl.multiple_of((col0 // LANES) * LANES, LANES), LANES)
    to_lane0 = (LANES - col0 % LANES) % LANES
    lane = lax.broadcasted_iota(jnp.int32, (HEAD_DIM, LANES), 1)
    keep = lane < LANES - nt
    c_refs = (c0_ref, c1_ref, c2_ref)
    o_refs = (o0_ref, o1_ref, o2_ref)
    bias_refs = (b0_ref, b1_ref, b2_ref)
    nq = N_GROUPS * ATTN_OUT

    outs = []
    for hl in range(hc):
        h = pl.program_id(1) * hc + hl
        stats = []
        for g in range(N_GROUPS):
            wb = c_refs[g].shape[-1]
            ntile = wb // LANES

            def cols(row0):
                rows = pl.ds(pl.multiple_of(row0 + h * HEAD_DIM, HEAD_DIM), HEAD_DIM)
                return pltpu.roll(cols_ref[rows, tile], to_lane0, axis=1)

            q_t = cols(g * ATTN_OUT)
            q8 = jnp.concatenate([q_t, jnp.zeros_like(q_t)], axis=0).T[0:8, 0:HEAD_DIM].astype(BF16)
            new = (cols(nq + g * 2 * ATTN_OUT), cols(nq + g * 2 * ATTN_OUT + ATTN_OUT))
            old = (c_refs[g][0, hl], c_refs[g][1, hl])
            for kv in range(2):
                rot = [pltpu.roll(old[kv][:, j * LANES:(j + 1) * LANES], LANES - nt, axis=1) for j in range(ntile)]
                rot.append(pltpu.roll(new[kv], LANES - nt, axis=1))
                for j in range(ntile):
                    o_refs[g][kv, hl, :, j * LANES:(j + 1) * LANES] = jnp.where(keep, rot[j], rot[j + 1])
            s = jnp.dot(q8, old[0].astype(BF16), preferred_element_type=F32) + bias_refs[g][h]
            sn = jnp.dot(q8, new[0].astype(BF16), preferred_element_type=F32) + bn_ref[g, h]
            m = jnp.maximum(jnp.max(s, axis=1, keepdims=True), jnp.max(sn, axis=1, keepdims=True))
            e = jnp.exp(s - m)
            en = jnp.exp(sn - m)
            l = jnp.sum(e, axis=1, keepdims=True) + jnp.sum(en, axis=1, keepdims=True)
            acc = (lax.dot_general(e.astype(BF16), old[1].astype(BF16), _NT, preferred_element_type=F32)
                   + lax.dot_general(en.astype(BF16), new[1].astype(BF16), _NT, preferred_element_type=F32))
            stats.append((m, l, acc))
        mm = functools.reduce(jnp.maximum, [st[0] for st in stats])
        num = 0.0
        den = 0.0
        for m, l, acc in stats:
            w = jnp.exp(m - mm)
            num = num + w * acc
            den = den + w * l
        outs.append(num / den)
    att_ref[...] = jnp.concatenate(outs, axis=1)


def _sample_attention(cols, caches_t, nb, nt, hc=4):
    cbias, nbias = _sample_bias(nt)
    for g, (win, _) in enumerate(GROUPS):
        assert caches_t[g].shape[-1] == win, "window buffer must hold a full window"
    assert nt <= 8 and LANES % nt == 0 and (nb * nt) % LANES == 0
    cspec = lambda wb: pl.BlockSpec((None, 2, hc, HEAD_DIM, wb), lambda i, j: (i, 0, j, 0, 0))
    cspecs = [cspec(c.shape[-1]) for c in caches_t]
    res = pl.pallas_call(
        functools.partial(_sample_body, nt=nt, hc=hc),
        grid=(nb, N_SLOTS // hc),
        in_specs=[_resident()] + cspecs + [_resident()] * (N_GROUPS + 1),
        out_specs=cspecs + [pl.BlockSpec((None, 8, hc * HEAD_DIM), lambda i, j: (i, 0, j))],
        out_shape=[jax.ShapeDtypeStruct(c.shape, F32) for c in caches_t]
                  + [jax.ShapeDtypeStruct((nb, 8, ATTN_OUT), F32)],
        compiler_params=_params(("arbitrary", "arbitrary")),
        name="sample_attention",
    )(cols, *caches_t, *cbias, nbias)
    return res[:N_GROUPS], res[N_GROUPS]


def _mix_body(*refs, combine):
    if combine:
        stats = [refs[3 * g:3 * g + 3] for g in range(N_GROUPS)]
        rest = refs[3 * N_GROUPS:]
        mm = functools.reduce(jnp.maximum, [st[1][...] for st in stats])
        num = 0.0
        den = 0.0
        for a_ref, m_ref, l_ref in stats:
            w = jnp.exp(m_ref[...] - mm)
            num = num + w * a_ref[...]
            den = den + w * l_ref[...]
        ob = num / den
    else:
        ob = refs[0][...]
        rest = refs[1:]
    gya_ref, sgb_ref, x_ref, wob_ref, wo_ref, g2_ref, x1_ref, h2_ref, h2t_ref = rest
    yb = jnp.dot(ob.astype(BF16), wob_ref[...], preferred_element_type=F32)
    mix = gya_ref[...] + sgb_ref[...] * yb
    x1 = x_ref[...] + jnp.dot(mix.astype(BF16), wo_ref[...], preferred_element_type=F32)
    x1_ref[...] = x1
    h2 = _rms(x1, g2_ref[...])
    h2_ref[...] = h2.astype(BF16)
    h2t_ref[...] = h2.T.astype(BF16)


def _mix(attn, gya, sgb, x, wob, wo, g2, tm=512):
    n, d = x.shape
    tm = min(tm, n)
    combine = len(attn) > 1
    tok = lambda w: pl.BlockSpec((tm, w), lambda i: (i, 0))
    return pl.pallas_call(
        functools.partial(_mix_body, combine=combine),
        grid=(n // tm,),
        in_specs=[tok(ATTN_OUT)] * len(attn) + [tok(d), tok(d), tok(d), _resident(), _resident(), _resident()],
        out_specs=[tok(d), tok(d), pl.BlockSpec((d, tm), lambda i: (0, i))],
        out_shape=[jax.ShapeDtypeStruct((n, d), F32), jax.ShapeDtypeStruct((n, d), BF16),
                   jax.ShapeDtypeStruct((d, n), BF16)],
        compiler_params=_params(("arbitrary",)),
        name="mix_combine" if combine else "mix",
    )(*attn, gya, sgb, x, wob, wo, g2)


def _sort16_pairs():
    n, pairs, p = PEER_TOPK, [], 1
    while p < n:
        k = p
        while k >= 1:
            for j in range(k % p, n - k, 2 * k):
                for i in range(min(k, n - j - k)):
                    if (i + j) // (2 * p) == (i + j + k) // (2 * p):
                        pairs.append((i + j, i + j + k))
            k //= 2
        p *= 2
    return pairs


def _exchange(x, i, j):
    x[i], x[j] = jnp.maximum(x[i], x[j]), jnp.minimum(x[i], x[j])


def _merge_top16(a, b):
    n = PEER_TOPK
    c = [jnp.maximum(a[i], b[n - 1 - i]) for i in range(n)]
    d = n // 2
    while d >= 1:
        for i in range(n):
            if i & d == 0:
                _exchange(c, i, i + d)
        d //= 2
    return c


def _merge_sublanes(x):
    for shift in (4, 2, 1):
        x = _merge_top16(x, [pltpu.roll(v, shift, axis=0) for v in x])
    return x


def _route_body(h2_ref, wq_ref, keys_ref, s_ref, ec_ref, tau_ref, vals_ref, *, tm, nh):
    q = jnp.dot(h2_ref[...], wq_ref[...], preferred_element_type=F32).astype(BF16)
    for hh in range(2 * nh):
        s_ref[hh] = lax.dot_general(keys_ref[hh], q[:, hh * PEER_NKEYS:(hh + 1) * PEER_NKEYS], _NT,
                                    preferred_element_type=F32)
    nchunk = tm // LANES
    pairs = _sort16_pairs()

    def top_keys(i, carry):
        hh = i // nchunk
        cs = pl.ds(pl.multiple_of((i % nchunk) * LANES, LANES), LANES)
        x = [s_ref[hh, 8 * k:8 * k + 8, cs] for k in range(PEER_NKEYS // 8)]
        for a, b in pairs:
            _exchange(x, a, b)
        x = _merge_sublanes(x)
        for k in range(PEER_TOPK):
            vals_ref[hh, k, :, cs] = x[k]
        return carry

    lax.fori_loop(0, 2 * nh * nchunk, top_keys, 0)

    r8 = lax.broadcasted_iota(jnp.int32, (8, LANES), 0)
    count = functools.reduce(lambda acc, a: jnp.where(r8 == a, PEER_TOPK // (a + 1), acc), range(8), r8 * 0)

    def top_pairs(i, carry):
        h = i // nchunk
        cs = pl.ds(pl.multiple_of((i % nchunk) * LANES, LANES), LANES)
        v1 = [vals_ref[2 * h, k, :, cs] for k in range(PEER_TOPK)]
        v2 = [vals_ref[2 * h + 1, k, :, cs] for k in range(PEER_TOPK)]
        v1col = functools.reduce(lambda acc, a: jnp.where(r8 == a, v1[a], acc), range(1, 8), v1[0])
        x = [jnp.where(count > k, v1col + v2[k], -jnp.inf) for k in range(PEER_TOPK)]
        x = _merge_sublanes(x)
        tail = [v1[8 + k] + v2[0] if k < 8 else jnp.full((8, LANES), -jnp.inf, F32) for k in range(PEER_TOPK)]
        sv = _merge_top16(x, tail)
        z = functools.reduce(lambda a, b: a + b, [jnp.exp(v - sv[0]) for v in sv])[0:1]
        tau_ref[h, :, cs] = sv[PEER_TOPK - 1][0:1]
        ec_ref[2 * h, :, cs] = jnp.exp(s_ref[2 * h, :, cs] - v1[0][0:1]) / z
        ec_ref[2 * h + 1, :, cs] = jnp.exp(s_ref[2 * h + 1, :, cs] - v2[0][0:1])
        return carry

    lax.fori_loop(0, nh * nchunk, top_pairs, 0)


def _route(h2, wq, keys, tm=512):
    n, d = h2.shape
    nh2 = keys.shape[0]
    tm = min(tm, n)
    big = pl.BlockSpec((nh2, PEER_NKEYS, tm), lambda i: (0, 0, i))
    return pl.pallas_call(
        functools.partial(_route_body, tm=tm, nh=nh2 // 2),
        grid=(n // tm,),
        in_specs=[pl.BlockSpec((tm, d), lambda i: (i, 0)), _resident(), _resident()],
        out_specs=[big, big, pl.BlockSpec((nh2 // 2, 1, tm), lambda i: (0, 0, i))],
        out_shape=[jax.ShapeDtypeStruct((nh2, PEER_NKEYS, n), F32), jax.ShapeDtypeStruct((nh2, PEER_NKEYS, n), F32),
                   jax.ShapeDtypeStruct((nh2 // 2, 1, n), F32)],
        scratch_shapes=[pltpu.VMEM((nh2, PEER_TOPK, 8, tm), F32)],
        compiler_params=_params(("arbitrary",)),
        name="peer_route",
    )(h2, wq, keys)


def _experts_body(h2t_ref, u_ref, vt_ref, s_ref, ec_ref, tau_ref, x1_ref, gf_ref, y_ref,
                  acc_ref, act0_ref, act1_ref, p0_ref, p1_ref, *, tm, eblk, nh):
    j = pl.program_id(1)
    half = tm // 2
    nsub = eblk // PEER_NKEYS
    halves = ((act0_ref, p0_ref, 0), (act1_ref, p1_ref, half))

    @pl.when(j == 0)
    def _():
        acc_ref[...] = jnp.zeros_like(acc_ref)

    for act_ref, _, off in halves:
        act_ref[...] = jnp.dot(u_ref[...], h2t_ref[:, off:off + half], preferred_element_type=F32)
    r8 = lax.broadcasted_iota(jnp.int32, (8, LANES), 0)
    for act_ref, p_ref, off in halves:
        for sub in range(nsub):
            i1 = j * nsub + sub
            grp = pl.ds(pl.multiple_of((i1 // 8) * 8, 8), 8)
            pick = r8 == (i1 % 8)
            rs = slice(sub * PEER_NKEYS, (sub + 1) * PEER_NKEYS)
            for c0 in range(0, half, LANES):
                cs = slice(off + c0, off + c0 + LANES)
                g = jnp.zeros((PEER_NKEYS, LANES), F32)
                for h in range(nh):
                    s1 = jnp.sum(jnp.where(pick, s_ref[2 * h, grp, cs], 0.0), axis=0, keepdims=True)
                    c1 = jnp.sum(jnp.where(pick, ec_ref[2 * h, grp, cs], 0.0), axis=0, keepdims=True)
                    pair = s_ref[2 * h + 1, :, cs] + s1
                    w = ec_ref[2 * h + 1, :, cs] * c1
                    g = g + jnp.where(pair >= tau_ref[h, :, cs], w, 0.0)
                a = act_ref[rs, c0:c0 + LANES]
                gelu = 0.5 * a * (1.0 + lax.erf(a * np.float32(np.sqrt(0.5))))
                p_ref[rs, c0:c0 + LANES] = (g * gelu).astype(BF16)
    for _, p_ref, off in halves:
        acc_ref[:, off:off + half] += jnp.dot(vt_ref[...], p_ref[...], preferred_element_type=F32)

    @pl.when(j == pl.num_programs(1) - 1)
    def _():
        y_ref[...] = _rms(x1_ref[...] + acc_ref[...].T, gf_ref[...])


def _experts(h2t, ub, vtb, s_all, ec_all, tau, x1, gf, tm=512, eblk=512):
    d, n = h2t.shape
    e = ub.shape[0]
    nh2 = s_all.shape[0]
    tm = min(tm, n)
    big = pl.BlockSpec((nh2, PEER_NKEYS, tm), lambda i, j: (0, 0, i))
    return pl.pallas_call(
        functools.partial(_experts_body, tm=tm, eblk=eblk, nh=nh2 // 2),
        grid=(n // tm, e // eblk),
        in_specs=[
            pl.BlockSpec((d, tm), lambda i, j: (0, i)),
            pl.BlockSpec((eblk, d), lambda i, j: (j, 0)),
            pl.BlockSpec((d, eblk), lambda i, j: (0, j)),
            big, big,
            pl.BlockSpec((nh2 // 2, 1, tm), lambda i, j: (0, 0, i)),
            pl.BlockSpec((tm, d), lambda i, j: (i, 0)),
            _resident(),
        ],
        out_specs=pl.BlockSpec((tm, d), lambda i, j: (i, 0)),
        out_shape=jax.ShapeDtypeStruct((n, d), F32),
        scratch_shapes=[pltpu.VMEM((d, tm), F32)] + [pltpu.VMEM((eblk, tm // 2), F32)] * 2
                       + [pltpu.VMEM((eblk, tm // 2), BF16)] * 2,
        compiler_params=_params(("arbitrary", "arbitrary")),
        name="peer_experts",
    )(h2t, ub, vtb, s_all, ec_all, tau, x1, gf)


def _arrange_w_in(w, c, d):
    na = N_GROUPS * ATTN_OUT
    o = 3 * c
    k = w[:, o + na:o + 2 * na]
    v = w[:, o + 2 * na:o + 3 * na]
    kv = []
    for g in range(N_GROUPS):
        kv += [k[:, g * ATTN_OUT:(g + 1) * ATTN_OUT], v[:, g * ATTN_OUT:(g + 1) * ATTN_OUT]]
    return jnp.concatenate([w[:, :o + na]] + kv + [w[:, o + 3 * na:]], axis=1).astype(BF16)


def _tail(attn, gya, sgb, x, wts):
    x1, h2, h2t = _mix(attn, gya, sgb, x, wts["wob"], wts["wo"], wts["g2"])
    s_all, ec_all, tau = _route(h2, wts["wq"], wts["keys"])
    return _experts(h2t, wts["ub"], wts["vtb"], s_all, ec_all, tau, x1, wts["gf"])


def kernel(x_prompt, x_sample, cache_kv_w128, cache_kv_w512, cache_kv_w2048, state_conv, norm1_g, w_in,
           conv_w, w_out_a, w_out_b, w_o, norm2_g, peer_wq, peer_keys, peer_u, peer_v, final_g):
    depth = w_in.shape[0]
    assert depth == 1, "single-layer trunk"
    b, s, d = x_prompt.shape
    nb, nt, _ = x_sample.shape
    c = conv_w.shape[-1]
    nh = peer_keys.shape[2]

    ub, vtb = _prep_tables(peer_u[0], peer_v[0])
    wts = dict(
        wob=w_out_b[0].astype(BF16), wo=w_o[0].astype(BF16), g2=norm2_g[0][None, :],
        wq=peer_wq[0].astype(BF16),
        keys=jnp.swapaxes(peer_keys[0], 0, 1).reshape(2 * nh, PEER_NKEYS, -1).astype(BF16),
        ub=ub, vtb=vtb, gf=final_g[None, :],
    )
    win = _arrange_w_in(w_in[0], c, d)
    woa = w_out_a[0].astype(BF16)
    g1 = norm1_g[0][None, :]
    cw = conv_w[0]

    q, kv0, kv1, kv2, gya, sgb, cst_p = _proj_prompt(x_prompt, g1, win, cw, woa)
    kvs = (kv0, kv1, kv2)
    attn = []
    for g in range(N_GROUPS):
        attn += list(_attn_prompt_group(q, kvs[g], g))
    n = b * s
    y_prompt = _tail(attn, gya.reshape(n, d), sgb.reshape(n, d), x_prompt.reshape(n, d), wts).reshape(b, s, d)
    to_rows_major = lambda a: jnp.transpose(a, (0, 4, 1, 2, 3))[None]
    kv_prompt = []
    for g, (win_g, _) in enumerate(GROUPS):
        keep = min(win_g, s)
        kt = _kv_transposed(kvs[g], keep)
        kv_prompt.append(to_rows_major(kt.reshape(b, 2, N_SLOTS, HEAD_DIM, keep)))

    x_tm = jnp.swapaxes(x_sample, 0, 1).reshape(nt * nb, d)
    st_tm = jnp.swapaxes(state_conv[0], 0, 1).reshape((CONV_WIDTH - 1) * nb, c)
    gya_s, sgb_s, cst_s, cols = _proj_sample(x_tm, x_sample.reshape(nb * nt, d), st_tm, g1, win, cw, woa, nb, nt)
    caches_t = [jnp.transpose(cv[0], (0, 2, 3, 4, 1)) for cv in (cache_kv_w128, cache_kv_w512, cache_kv_w2048)]
    new_caches, att = _sample_attention(cols, caches_t, nb, nt)
    ob_s = jnp.swapaxes(att[:, :nt, :], 0, 1).reshape(nt * nb, ATTN_OUT)
    y_s = _tail([ob_s], gya_s, sgb_s, x_tm, wts)
    y_sample = jnp.swapaxes(y_s.reshape(nt, nb, d), 0, 1)
    kv_sample = [to_rows_major(o) for o in new_caches]
    conv_sample = jnp.swapaxes(cst_s.reshape(CONV_WIDTH - 1, nb, c), 0, 1)[None]

    return (y_prompt, y_sample, kv_prompt[0], kv_prompt[1], kv_prompt[2], cst_p[None],
            kv_sample[0], kv_sample[1], kv_sample[2], conv_sample)
```

```python
import functools

import jax
import jax.numpy as jnp
import numpy as np
from jax import lax
from jax.experimental import pallas as pl
from jax.experimental.pallas import tpu as pltpu

F32 = jnp.float32
BF16 = jnp.bfloat16

HEAD_DIM = 64
N_SLOTS = 8
GROUPS = ((128, 1), (512, 4), (2048, 16))
N_GROUPS = len(GROUPS)
ATTN_OUT = N_SLOTS * HEAD_DIM
WIN = 128
CONV_WIDTH = 3
PEER_TOPK = 16
PEER_NKEYS = 128
EPS = 1e-6
NEG = -1e30

LANES = 128
VMEM_LIMIT = 56 * 1024 * 1024

_NT = (((1,), (1,)), ((), ()))


def _params(sem):
    return pltpu.CompilerParams(dimension_semantics=sem, vmem_limit_bytes=VMEM_LIMIT)


def _resident():
    return pl.BlockSpec(memory_space=pltpu.VMEM)


def _rms(x, g):
    return x * lax.rsqrt(jnp.mean(x * x, axis=-1, keepdims=True) + EPS) * g


def _slopes():
    return np.exp2(-8.0 * np.arange(1, N_SLOTS + 1, dtype=np.float64) / N_SLOTS)


def _prep_body(u_ref, v_ref, ub_ref, vt_ref):
    ub_ref[...] = u_ref[...].astype(BF16)
    vt_ref[...] = v_ref[...].T.astype(BF16)


def _prep_tables(u, v):
    e, d = u.shape
    blk = 512
    return pl.pallas_call(
        _prep_body,
        grid=(e // blk,),
        in_specs=[pl.BlockSpec((blk, d), lambda i: (i, 0)), pl.BlockSpec((blk, d), lambda i: (i, 0))],
        out_specs=[pl.BlockSpec((blk, d), lambda i: (i, 0)), pl.BlockSpec((d, blk), lambda i: (0, i))],
        out_shape=[jax.ShapeDtypeStruct((e, d), BF16), jax.ShapeDtypeStruct((d, e), BF16)],
        compiler_params=_params(("arbitrary",)),
        name="prep_tables",
    )(u, v)


def _proj_core(h, win_ref, woa_ref, conv_fn, c, d, q_ref, kv_refs, gya_ref, sgb_ref):
    def proj(a, n):
        return jnp.dot(h, win_ref[:, a:a + n], preferred_element_type=F32)

    o_q = 3 * c
    o_kv = o_q + N_GROUPS * ATTN_OUT
    o_ga = o_kv + N_GROUPS * 2 * ATTN_OUT
    o_gb = o_ga + d
    u = proj(c, c) * proj(2 * c, c)
    y = conv_fn(u)
    ya = jnp.dot((proj(0, c) * y).astype(BF16), woa_ref[...], preferred_element_type=F32)
    gya_ref[...] = jax.nn.sigmoid(proj(o_ga, d)) * ya
    sgb_ref[...] = jax.nn.sigmoid(proj(o_gb, d))
    if q_ref is not None:
        q_ref[...] = proj(o_q, N_GROUPS * ATTN_OUT) * (HEAD_DIM ** -0.5)
        for g in range(N_GROUPS):
            kv_refs[g][...] = proj(o_kv + g * 2 * ATTN_OUT, 2 * ATTN_OUT)


def _proj_prompt_body(x_ref, g1_ref, win_ref, cw_ref, woa_ref,
                      q_ref, kv0_ref, kv1_ref, kv2_ref, gya_ref, sgb_ref, cst_ref, uext_ref, *, tm, c, d):
    j = pl.program_id(1)
    pre = 8

    @pl.when(j == 0)
    def _():
        uext_ref[0:pre, :] = jnp.zeros((pre, c), F32)

    def conv(u):
        uext_ref[pre:pre + tm, :] = u
        cw = cw_ref[...]
        return (cw[0:1] * uext_ref[pre - 2:pre - 2 + tm, :]
                + cw[1:2] * uext_ref[pre - 1:pre - 1 + tm, :]
                + cw[2:3] * u)

    h = _rms(x_ref[...], g1_ref[...]).astype(BF16)
    _proj_core(h, win_ref, woa_ref, conv, c, d, q_ref, (kv0_ref, kv1_ref, kv2_ref), gya_ref, sgb_ref)
    cst_ref[...] = uext_ref[pre + tm - 2:pre + tm, :]
    uext_ref[0:pre, :] = uext_ref[tm:tm + pre, :]


def _proj_prompt(x, g1, win, cw, woa, tm=256):
    b, s, d = x.shape
    c = cw.shape[1]
    nq = N_GROUPS * ATTN_OUT
    tok = lambda n: pl.BlockSpec((None, tm, n), lambda i, j: (i, j, 0))
    outs = [jax.ShapeDtypeStruct((b, s, nq), F32)]
    outs += [jax.ShapeDtypeStruct((b, s, 2 * ATTN_OUT), F32)] * N_GROUPS
    outs += [jax.ShapeDtypeStruct((b, s, d), F32)] * 2
    outs += [jax.ShapeDtypeStruct((b, CONV_WIDTH - 1, c), F32)]
    return pl.pallas_call(
        functools.partial(_proj_prompt_body, tm=tm, c=c, d=d),
        grid=(b, s // tm),
        in_specs=[tok(d), _resident(), _resident(), _resident(), _resident()],
        out_specs=[tok(nq)] + [tok(2 * ATTN_OUT)] * N_GROUPS + [tok(d), tok(d),
                   pl.BlockSpec((None, CONV_WIDTH - 1, c), lambda i, j: (i, 0, 0))],
        out_shape=outs,
        scratch_shapes=[pltpu.VMEM((tm + 8, c), F32)],
        compiler_params=_params(("arbitrary", "arbitrary")),
        name="proj_prompt",
    )(x, g1, win, cw, woa)


def _proj_sample_body(x_ref, xb_ref, st_ref, g1_ref, win_ref, cw_ref, woa_ref,
                      gya_ref, sgb_ref, cst_ref, cols_ref, uext_ref, *, nb, nt, c, d):
    npre = (CONV_WIDTH - 1) * nb
    o_q = 3 * c
    nqkv = 3 * N_GROUPS * ATTN_OUT
    hb = _rms(xb_ref[...], g1_ref[...]).astype(BF16)
    qkv = jnp.dot(hb, win_ref[:, o_q:o_q + nqkv], preferred_element_type=F32)
    nq = N_GROUPS * ATTN_OUT
    cols_ref[0:nq, :] = (qkv[:, :nq] * (HEAD_DIM ** -0.5)).T
    cols_ref[nq:, :] = qkv[:, nq:].T

    def conv(u):
        uext_ref[0:npre, :] = st_ref[...]
        uext_ref[npre:npre + nt * nb, :] = u
        cw = cw_ref[...]
        return (cw[0:1] * uext_ref[0:nt * nb, :]
                + cw[1:2] * uext_ref[nb:nb + nt * nb, :]
                + cw[2:3] * u)

    h = _rms(x_ref[...], g1_ref[...]).astype(BF16)
    _proj_core(h, win_ref, woa_ref, conv, c, d, None, None, gya_ref, sgb_ref)
    cst_ref[...] = uext_ref[nt * nb:nt * nb + npre, :]


def _proj_sample(x_tm, x_bm, st_tm, g1, win, cw, woa, nb, nt):
    n, d = x_tm.shape
    c = cw.shape[1]
    npre = (CONV_WIDTH - 1) * nb
    outs = [jax.ShapeDtypeStruct((n, d), F32)] * 2
    outs += [jax.ShapeDtypeStruct((npre, c), F32)]
    outs += [jax.ShapeDtypeStruct((3 * N_GROUPS * ATTN_OUT, n), F32)]
    return pl.pallas_call(
        functools.partial(_proj_sample_body, nb=nb, nt=nt, c=c, d=d),
        in_specs=[_resident()] * 7,
        out_specs=[_resident()] * 4,
        out_shape=outs,
        scratch_shapes=[pltpu.VMEM((npre + n, c), F32)],
        compiler_params=pltpu.CompilerParams(vmem_limit_bytes=VMEM_LIMIT),
        name="proj_sample",
    )(x_tm, x_bm, st_tm, g1, win, cw, woa)


def _kvt_body(kv_ref, out_ref):
    out_ref[...] = kv_ref[...].T


def _kv_transposed(kv, keep, tm=256):
    b, s, w = kv.shape
    tm = min(tm, keep)
    off = (s - keep) // tm
    return pl.pallas_call(
        _kvt_body,
        grid=(b, keep // tm),
        in_specs=[pl.BlockSpec((None, tm, w), lambda i, j: (i, j + off, 0))],
        out_specs=pl.BlockSpec((None, w, tm), lambda i, j: (i, 0, j)),
        out_shape=jax.ShapeDtypeStruct((b, w, keep), F32),
        compiler_params=_params(("arbitrary", "arbitrary")),
        name="kv_transposed",
    )(kv)


def _prompt_bias(dil):
    tq = np.arange(WIN)[:, None]
    col = np.arange(2 * WIN)[None, :]
    dist = tq + WIN - col
    ok = (dist >= 0) & (dist <= WIN)
    b = -_slopes()[:, None, None] * (dil * dist)[None]
    return jnp.asarray(np.where(ok[None], b, NEG), F32)


def _attn_prompt_body(*refs, dil, npair, has_prev):
    if has_prev:
        q_ref, k_ref, v_ref, kp_ref, vp_ref, bias_ref, acc_ref, m_ref, l_ref = refs
    else:
        q_ref, k_ref, v_ref, bias_ref, acc_ref, m_ref, l_ref = refs
        kp_ref, vp_ref = k_ref, v_ref
    first = pl.program_id(2) == 0
    pair0 = pl.program_id(1) * npair
    lo = lax.broadcasted_iota(jnp.int32, (WIN, LANES), 1) < HEAD_DIM
    col = lax.broadcasted_iota(jnp.int32, (WIN, 2 * WIN), 1)
    kill = jnp.logical_and(col < WIN, first)
    for p in range(npair):
        ls = slice(p * LANES, (p + 1) * LANES)
        for r in range(dil):
            rows = slice(None) if dil == 1 else pl.ds(r, WIN, stride=dil)
            qp = q_ref[rows, ls]
            kcat = jnp.concatenate([kp_ref[rows, ls], k_ref[rows, ls]], axis=0).astype(BF16)
            vcat = jnp.concatenate([vp_ref[rows, ls], v_ref[rows, ls]], axis=0).astype(BF16)
            res = []
            for hh in range(2):
                qm = jnp.where(lo if hh == 0 else jnp.logical_not(lo), qp, 0.0).astype(BF16)
                s = lax.dot_general(qm, kcat, _NT, preferred_element_type=F32)
                s = jnp.where(kill, NEG, s + bias_ref[2 * (pair0 + p) + hh])
                m = jnp.max(s, axis=-1, keepdims=True)
                e = jnp.exp(s - m)
                l = jnp.sum(e, axis=-1, keepdims=True)
                pv = jnp.dot(e.astype(BF16), vcat, preferred_element_type=F32)
                res.append((m, l, pv))
            acc_ref[rows, ls] = jnp.where(lo, res[0][2], res[1][2])
            m_ref[rows, ls] = jnp.where(lo, res[0][0], res[1][0])
            l_ref[rows, ls] = jnp.where(lo, res[0][1], res[1][1])


def _attn_prompt_group(q, kv, g):
    b, s, nq = q.shape
    dil = GROUPS[g][1]
    span = WIN * dil
    nspan = s // span
    has_prev = nspan > 1
    npair = 4 if dil == 1 else 1
    w = npair * LANES
    nw = ATTN_OUT // w
    cur = lambda off: pl.BlockSpec((None, span, w), lambda i, p, j: (i, j, off + p))
    prev = lambda off: pl.BlockSpec((None, span, w), lambda i, p, j: (i, jnp.maximum(j - 1, 0), off + p))
    in_specs = [cur(g * nw), cur(0), cur(nw)] + ([prev(0), prev(nw)] if has_prev else []) + [_resident()]
    out_sd = jax.ShapeDtypeStruct((b, s, ATTN_OUT), F32)
    acc, m, l = pl.pallas_call(
        functools.partial(_attn_prompt_body, dil=dil, npair=npair, has_prev=has_prev),
        grid=(b, nw, nspan),
        in_specs=in_specs,
        out_specs=[cur(0)] * 3,
        out_shape=[out_sd] * 3,
        compiler_params=_params(("arbitrary", "arbitrary", "arbitrary")),
        name=f"attn_prompt_g{g}",
    )(q, kv, kv, *([kv, kv] if has_prev else []), _prompt_bias(dil))
    shp = (b * s, ATTN_OUT)
    return acc.reshape(shp), m.reshape(shp), l.reshape(shp)


def _sample_bias(nt):
    sl = _slopes()[:, None, None]
    cache, new = [], []
    for win, dil in GROUPS:
        t = np.arange(8)[None, :, None]
        dist = win + t - np.arange(win)[None, None, :]
        ok = (dist % dil == 0) & (dist <= win)
        cache.append(jnp.asarray(np.where(t < nt, np.where(ok, -sl * dist, NEG), 0.0), F32))
        dn = t - np.arange(LANES)[None, None, :]
        okn = (dn >= 0) & (dn % dil == 0)
        new.append(np.where(t < nt, np.where(okn, -sl * dn, NEG), 0.0))
    return cache, jnp.asarray(np.stack(new), F32)


def _sample_body(cols_ref, c0_ref, c1_ref, c2_ref, b0_ref, b1_ref, b2_ref, bn_ref,
                 o0_ref, o1_ref, o2_ref, att_ref, *, nt, hc):
    b = pl.program_id(0)
    col0 = b * nt
    tile = pl.ds(pl.multiple_of((col0 // LANES) * LANES, LANES), LANES)
    to_lane0 = (LANES - col0 % LANES) % LANES
    lane = lax.broadcasted_iota(jnp.int32, (HEAD_DIM, LANES), 1)
    keep = lane < LANES - nt
    c_refs = (c0_ref, c1_ref, c2_ref)
    o_refs = (o0_ref, o1_ref, o2_ref)
    bias_refs = (b0_ref, b1_ref, b2_ref)
    nq = N_GROUPS * ATTN_OUT

    outs = []
    for hl in range(hc):
        h = pl.program_id(1) * hc + hl
        stats = []
        for g in range(N_GROUPS):
            wb = c_refs[g].shape[-1]
            ntile = wb // LANES

            def cols(row0):
                rows = pl.ds(pl.multiple_of(row0 + h * HEAD_DIM, HEAD_DIM), HEAD_DIM)
                return pltpu.roll(cols_ref[rows, tile], to_lane0, axis=1)

            q_t = cols(g * ATTN_OUT)
            q8 = jnp.concatenate([q_t, jnp.zeros_like(q_t)], axis=0).T[0:8, 0:HEAD_DIM].astype(BF16)
            new = (cols(nq + g * 2 * ATTN_OUT), cols(nq + g * 2 * ATTN_OUT + ATTN_OUT))
            old = (c_refs[g][0, hl], c_refs[g][1, hl])
            for kv in range(2):
                rot = [pltpu.roll(old[kv][:, j * LANES:(j + 1) * LANES], LANES - nt, axis=1) for j in range(ntile)]
                rot.append(pltpu.roll(new[kv], LANES - nt, axis=1))
                for j in range(ntile):
                    o_refs[g][kv, hl, :, j * LANES:(j + 1) * LANES] = jnp.where(keep, rot[j], rot[j + 1])
            s = jnp.dot(q8, old[0].astype(BF16), preferred_element_type=F32) + bias_refs[g][h]
            sn = jnp.dot(q8, new[0].astype(BF16), preferred_element_type=F32) + bn_ref[g, h]
            m = jnp.maximum(jnp.max(s, axis=1, keepdims=True), jnp.max(sn, axis=1, keepdims=True))
            e = jnp.exp(s - m)
            en = jnp.exp(sn - m)
            l = jnp.sum(e, axis=1, keepdims=True) + jnp.sum(en, axis=1, keepdims=True)
            acc = (lax.dot_general(e.astype(BF16), old[1].astype(BF16), _NT, preferred_element_type=F32)
                   + lax.dot_general(en.astype(BF16), new[1].astype(BF16), _NT, preferred_element_type=F32))
            stats.append((m, l, acc))
        mm = functools.reduce(jnp.maximum, [st[0] for st in stats])
        num = 0.0
        den = 0.0
        for m, l, acc in stats:
            w = jnp.exp(m - mm)
            num = num + w * acc
            den = den + w * l
        outs.append(num / den)
    att_ref[...] = jnp.concatenate(outs, axis=1)


def _sample_attention(cols, caches_t, nb, nt, hc=4):
    cbias, nbias = _sample_bias(nt)
    for g, (win, _) in enumerate(GROUPS):
        assert caches_t[g].shape[-1] == win, "window buffer must hold a full window"
    assert nt <= 8 and LANES % nt == 0 and (nb * nt) % LANES == 0
    cspec = lambda wb: pl.BlockSpec((None, 2, hc, HEAD_DIM, wb), lambda i, j: (i, 0, j, 0, 0))
    cspecs = [cspec(c.shape[-1]) for c in caches_t]
    res = pl.pallas_call(
        functools.partial(_sample_body, nt=nt, hc=hc),
        grid=(nb, N_SLOTS // hc),
        in_specs=[_resident()] + cspecs + [_resident()] * (N_GROUPS + 1),
        out_specs=cspecs + [pl.BlockSpec((None, 8, hc * HEAD_DIM), lambda i, j: (i, 0, j))],
        out_shape=[jax.ShapeDtypeStruct(c.shape, F32) for c in caches_t]
                  + [jax.ShapeDtypeStruct((nb, 8, ATTN_OUT), F32)],
        compiler_params=_params(("arbitrary", "arbitrary")),
        name="sample_attention",
    )(cols, *caches_t, *cbias, nbias)
    return res[:N_GROUPS], res[N_GROUPS]


def _mix_body(*refs, combine):
    if combine:
        stats = [refs[3 * g:3 * g + 3] for g in range(N_GROUPS)]
        rest = refs[3 * N_GROUPS:]
        mm = functools.reduce(jnp.maximum, [st[1][...] for st in stats])
        num = 0.0
        den = 0.0
        for a_ref, m_ref, l_ref in stats:
            w = jnp.exp(m_ref[...] - mm)
            num = num + w * a_ref[...]
            den = den + w * l_ref[...]
        ob = num / den
    else:
        ob = refs[0][...]
        rest = refs[1:]
    gya_ref, sgb_ref, x_ref, wob_ref, wo_ref, g2_ref, x1_ref, h2_ref, h2t_ref = rest
    yb = jnp.dot(ob.astype(BF16), wob_ref[...], preferred_element_type=F32)
    mix = gya_ref[...] + sgb_ref[...] * yb
    x1 = x_ref[...] + jnp.dot(mix.astype(BF16), wo_ref[...], preferred_element_type=F32)
    x1_ref[...] = x1
    h2 = _rms(x1, g2_ref[...])
    h2_ref[...] = h2.astype(BF16)
    h2t_ref[...] = h2.T.astype(BF16)


def _mix(attn, gya, sgb, x, wob, wo, g2, tm=512):
    n, d = x.shape
    tm = min(tm, n)
    combine = len(attn) > 1
    tok = lambda w: pl.BlockSpec((tm, w), lambda i: (i, 0))
    return pl.pallas_call(
        functools.partial(_mix_body, combine=combine),
        grid=(n // tm,),
        in_specs=[tok(ATTN_OUT)] * len(attn) + [tok(d), tok(d), tok(d), _resident(), _resident(), _resident()],
        out_specs=[tok(d), tok(d), pl.BlockSpec((d, tm), lambda i: (0, i))],
        out_shape=[jax.ShapeDtypeStruct((n, d), F32), jax.ShapeDtypeStruct((n, d), BF16),
                   jax.ShapeDtypeStruct((d, n), BF16)],
        compiler_params=_params(("arbitrary",)),
        name="mix_combine" if combine else "mix",
    )(*attn, gya, sgb, x, wob, wo, g2)


def _sort16_pairs():
    n, pairs, p = PEER_TOPK, [], 1
    while p < n:
        k = p
        while k >= 1:
            for j in range(k % p, n - k, 2 * k):
                for i in range(min(k, n - j - k)):
                    if (i + j) // (2 * p) == (i + j + k) // (2 * p):
                        pairs.append((i + j, i + j + k))
            k //= 2
        p *= 2
    return pairs


def _exchange(x, i, j):
    x[i], x[j] = jnp.maximum(x[i], x[j]), jnp.minimum(x[i], x[j])


def _merge_top16(a, b):
    n = PEER_TOPK
    c = [jnp.maximum(a[i], b[n - 1 - i]) for i in range(n)]
    d = n // 2
    while d >= 1:
        for i in range(n):
            if i & d == 0:
                _exchange(c, i, i + d)
        d //= 2
    return c


def _merge_sublanes(x):
    for shift in (4, 2, 1):
        x = _merge_top16(x, [pltpu.roll(v, shift, axis=0) for v in x])
    return x


def _route_body(h2_ref, wq_ref, keys_ref, s_ref, ec_ref, tau_ref, vals_ref, *, tm, nh):
    q = jnp.dot(h2_ref[...], wq_ref[...], preferred_element_type=F32).astype(BF16)
    for hh in range(2 * nh):
        s_ref[hh] = lax.dot_general(keys_ref[hh], q[:, hh * PEER_NKEYS:(hh + 1) * PEER_NKEYS], _NT,
                                    preferred_element_type=F32)
    nchunk = tm // LANES
    pairs = _sort16_pairs()

    def top_keys(i, carry):
        hh = i // nchunk
        cs = pl.ds(pl.multiple_of((i % nchunk) * LANES, LANES), LANES)
        x = [s_ref[hh, 8 * k:8 * k + 8, cs] for k in range(PEER_NKEYS // 8)]
        for a, b in pairs:
            _exchange(x, a, b)
        x = _merge_sublanes(x)
        for k in range(PEER_TOPK):
            vals_ref[hh, k, :, cs] = x[k]
        return carry

    lax.fori_loop(0, 2 * nh * nchunk, top_keys, 0)

    r8 = lax.broadcasted_iota(jnp.int32, (8, LANES), 0)
    count = functools.reduce(lambda acc, a: jnp.where(r8 == a, PEER_TOPK // (a + 1), acc), range(8), r8 * 0)

    def top_pairs(i, carry):
        h = i // nchunk
        cs = pl.ds(pl.multiple_of((i % nchunk) * LANES, LANES), LANES)
        v1 = [vals_ref[2 * h, k, :, cs] for k in range(PEER_TOPK)]
        v2 = [vals_ref[2 * h + 1, k, :, cs] for k in range(PEER_TOPK)]
        v1col = functools.reduce(lambda acc, a: jnp.where(r8 == a, v1[a], acc), range(1, 8), v1[0])
        x = [jnp.where(count > k, v1col + v2[k], -jnp.inf) for k in range(PEER_TOPK)]
        x = _merge_sublanes(x)
        tail = [v1[8 + k] + v2[0] if k < 8 else jnp.full((8, LANES), -jnp.inf, F32) for k in range(PEER_TOPK)]
        sv = _merge_top16(x, tail)
        z = functools.reduce(lambda a, b: a + b, [jnp.exp(v - sv[0]) for v in sv])[0:1]
        tau_ref[h, :, cs] = sv[PEER_TOPK - 1][0:1]
        ec_ref[2 * h, :, cs] = jnp.exp(s_ref[2 * h, :, cs] - v1[0][0:1]) / z
        ec_ref[2 * h + 1, :, cs] = jnp.exp(s_ref[2 * h + 1, :, cs] - v2[0][0:1])
        return carry

    lax.fori_loop(0, nh * nchunk, top_pairs, 0)


def _route(h2, wq, keys, tm=512):
    n, d = h2.shape
    nh2 = keys.shape[0]
    tm = min(tm, n)
    big = pl.BlockSpec((nh2, PEER_NKEYS, tm), lambda i: (0, 0, i))
    return pl.pallas_call(
        functools.partial(_route_body, tm=tm, nh=nh2 // 2),
        grid=(n // tm,),
        in_specs=[pl.BlockSpec((tm, d), lambda i: (i, 0)), _resident(), _resident()],
        out_specs=[big, big, pl.BlockSpec((nh2 // 2, 1, tm), lambda i: (0, 0, i))],
        out_shape=[jax.ShapeDtypeStruct((nh2, PEER_NKEYS, n), F32), jax.ShapeDtypeStruct((nh2, PEER_NKEYS, n), F32),
                   jax.ShapeDtypeStruct((nh2 // 2, 1, n), F32)],
        scratch_shapes=[pltpu.VMEM((nh2, PEER_TOPK, 8, tm), F32)],
        compiler_params=_params(("arbitrary",)),
        name="peer_route",
    )(h2, wq, keys)


def _gate_half(grp, col0, act_ref, p_ref, s_ref, ec_ref, tau_ref, *, half, eblk, nh):
    rows = PEER_NKEYS // 2
    for sub in range(eblk // PEER_NKEYS):
        for c0 in range(0, half, LANES):
            cs = slice(col0 + c0, col0 + c0 + LANES)
            for k0 in range(0, PEER_NKEYS, rows):
                ks = slice(k0, k0 + rows)
                g = jnp.zeros((rows, LANES), F32)
                for h in range(nh):
                    s1 = s_ref[2 * h, grp, cs][sub:sub + 1]
                    c1 = ec_ref[2 * h, grp, cs][sub:sub + 1]
                    pair = s_ref[2 * h + 1, ks, cs] + s1
                    w = ec_ref[2 * h + 1, ks, cs] * c1
                    g = g + jnp.where(pair >= tau_ref[h, :, cs], w, 0.0)
                rs = slice(sub * PEER_NKEYS + k0, sub * PEER_NKEYS + k0 + rows)
                a = act_ref[rs, c0:c0 + LANES]
                gelu = 0.5 * a * (1.0 + lax.erf(a * np.float32(np.sqrt(0.5))))
                p_ref[rs, c0:c0 + LANES] = (g * gelu).astype(BF16)


def _experts_body(h2t_ref, u_ref, vt_ref, s_ref, ec_ref, tau_ref, x1_ref, gf_ref, y_ref,
                  acc_ref, act0_ref, act1_ref, p0_ref, p1_ref, *, tm, eblk, nh):
    j = pl.program_id(1)
    half = tm // 2
    halves = ((act0_ref, p0_ref, 0), (act1_ref, p1_ref, half))
    grp = pl.ds(pl.multiple_of(j * 8, 8), 8)

    @pl.when(j == 0)
    def _():
        acc_ref[...] = jnp.zeros_like(acc_ref)

    for act_ref, _, off in halves:
        act_ref[...] = jnp.dot(u_ref[...], h2t_ref[:, off:off + half], preferred_element_type=F32)
    for act_ref, p_ref, off in halves:
        _gate_half(grp, off, act_ref, p_ref, s_ref, ec_ref, tau_ref, half=half, eblk=eblk, nh=nh)
    for _, p_ref, off in halves:
        acc_ref[:, off:off + half] += jnp.dot(vt_ref[...], p_ref[...], preferred_element_type=F32)

    @pl.when(j == pl.num_programs(1) - 1)
    def _():
        y_ref[...] = _rms(x1_ref[...] + acc_ref[...].T, gf_ref[...])


def _experts(h2t, ub, vtb, s_all, ec_all, tau, x1, gf, tm=512):
    d, n = h2t.shape
    e = ub.shape[0]
    nh2 = s_all.shape[0]
    tm = min(tm, n)
    eblk = 8 * PEER_NKEYS
    big = pl.BlockSpec((nh2, PEER_NKEYS, tm), lambda i, j: (0, 0, i))
    return pl.pallas_call(
        functools.partial(_experts_body, tm=tm, eblk=eblk, nh=nh2 // 2),
        grid=(n // tm, e // eblk),
        in_specs=[
            pl.BlockSpec((d, tm), lambda i, j: (0, i)),
            pl.BlockSpec((eblk, d), lambda i, j: (j, 0)),
            pl.BlockSpec((d, eblk), lambda i, j: (0, j)),
            big, big,
            pl.BlockSpec((nh2 // 2, 1, tm), lambda i, j: (0, 0, i)),
            pl.BlockSpec((tm, d), lambda i, j: (i, 0)),
            _resident(),
        ],
        out_specs=pl.BlockSpec((tm, d), lambda i, j: (i, 0)),
        out_shape=jax.ShapeDtypeStruct((n, d), F32),
        scratch_shapes=[pltpu.VMEM((d, tm), F32)] + [pltpu.VMEM((eblk, tm // 2), F32)] * 2
                       + [pltpu.VMEM((eblk, tm // 2), BF16)] * 2,
        compiler_params=_params(("arbitrary", "arbitrary")),
        name="peer_experts",
    )(h2t, ub, vtb, s_all, ec_all, tau, x1, gf)


def _arrange_w_in(w, c, d):
    na = N_GROUPS * ATTN_OUT
    o = 3 * c
    k = w[:, o + na:o + 2 * na]
    v = w[:, o + 2 * na:o + 3 * na]
    kv = []
    for g in range(N_GROUPS):
        kv += [k[:, g * ATTN_OUT:(g + 1) * ATTN_OUT], v[:, g * ATTN_OUT:(g + 1) * ATTN_OUT]]
    return jnp.concatenate([w[:, :o + na]] + kv + [w[:, o + 3 * na:]], axis=1).astype(BF16)


def _tail(attn, gya, sgb, x, wts):
    x1, h2, h2t = _mix(attn, gya, sgb, x, wts["wob"], wts["wo"], wts["g2"])
    s_all, ec_all, tau = _route(h2, wts["wq"], wts["keys"])
    return _experts(h2t, wts["ub"], wts["vtb"], s_all, ec_all, tau, x1, wts["gf"])


def kernel(x_prompt, x_sample, cache_kv_w128, cache_kv_w512, cache_kv_w2048, state_conv, norm1_g, w_in,
           conv_w, w_out_a, w_out_b, w_o, norm2_g, peer_wq, peer_keys, peer_u, peer_v, final_g):
    depth = w_in.shape[0]
    assert depth == 1, "single-layer trunk"
    b, s, d = x_prompt.shape
    nb, nt, _ = x_sample.shape
    c = conv_w.shape[-1]
    nh = peer_keys.shape[2]

    ub, vtb = _prep_tables(peer_u[0], peer_v[0])
    wts = dict(
        wob=w_out_b[0].astype(BF16), wo=w_o[0].astype(BF16), g2=norm2_g[0][None, :],
        wq=peer_wq[0].astype(BF16),
        keys=jnp.swapaxes(peer_keys[0], 0, 1).reshape(2 * nh, PEER_NKEYS, -1).astype(BF16),
        ub=ub, vtb=vtb, gf=final_g[None, :],
    )
    win = _arrange_w_in(w_in[0], c, d)
    woa = w_out_a[0].astype(BF16)
    g1 = norm1_g[0][None, :]
    cw = conv_w[0]

    q, kv0, kv1, kv2, gya, sgb, cst_p = _proj_prompt(x_prompt, g1, win, cw, woa)
    kvs = (kv0, kv1, kv2)
    attn = []
    for g in range(N_GROUPS):
        attn += list(_attn_prompt_group(q, kvs[g], g))
    n = b * s
    y_prompt = _tail(attn, gya.reshape(n, d), sgb.reshape(n, d), x_prompt.reshape(n, d), wts).reshape(b, s, d)
    to_rows_major = lambda a: jnp.transpose(a, (0, 4, 1, 2, 3))[None]
    kv_prompt = []
    for g, (win_g, _) in enumerate(GROUPS):
        keep = min(win_g, s)
        kt = _kv_transposed(kvs[g], keep)
        kv_prompt.append(to_rows_major(kt.reshape(b, 2, N_SLOTS, HEAD_DIM, keep)))

    x_tm = jnp.swapaxes(x_sample, 0, 1).reshape(nt * nb, d)
    st_tm = jnp.swapaxes(state_conv[0], 0, 1).reshape((CONV_WIDTH - 1) * nb, c)
    gya_s, sgb_s, cst_s, cols = _proj_sample(x_tm, x_sample.reshape(nb * nt, d), st_tm, g1, win, cw, woa, nb, nt)
    caches_t = [jnp.transpose(cv[0], (0, 2, 3, 4, 1)) for cv in (cache_kv_w128, cache_kv_w512, cache_kv_w2048)]
    new_caches, att = _sample_attention(cols, caches_t, nb, nt)
    ob_s = jnp.swapaxes(att[:, :nt, :], 0, 1).reshape(nt * nb, ATTN_OUT)
    y_s = _tail([ob_s], gya_s, sgb_s, x_tm, wts)
    y_sample = jnp.swapaxes(y_s.reshape(nt, nb, d), 0, 1)
    kv_sample = [to_rows_major(o) for o in new_caches]
    conv_sample = jnp.swapaxes(cst_s.reshape(CONV_WIDTH - 1, nb, c), 0, 1)[None]

    return (y_prompt, y_sample, kv_prompt[0], kv_prompt[1], kv_prompt[2], cst_p[None],
            kv_sample[0], kv_sample[1], kv_sample[2], conv_sample)
```

```python
import functools

import jax
import jax.numpy as jnp
import numpy as np
from jax import lax
from jax.experimental import pallas as pl
from jax.experimental.pallas import tpu as pltpu

F32 = jnp.float32
BF16 = jnp.bfloat16

HEAD_DIM = 64
N_SLOTS = 8
GROUPS = ((128, 1), (512, 4), (2048, 16))
N_GROUPS = len(GROUPS)
ATTN_OUT = N_SLOTS * HEAD_DIM
WIN = 128
CONV_WIDTH = 3
PEER_TOPK = 16
PEER_NKEYS = 128
EPS = 1e-6
NEG = -1e30

LANES = 128
VMEM_LIMIT = 56 * 1024 * 1024

_NT = (((1,), (1,)), ((), ()))


def _params(sem):
    return pltpu.CompilerParams(dimension_semantics=sem, vmem_limit_bytes=VMEM_LIMIT)


def _resident():
    return pl.BlockSpec(memory_space=pltpu.VMEM)


def _rms(x, g):
    return x * lax.rsqrt(jnp.mean(x * x, axis=-1, keepdims=True) + EPS) * g


def _slopes():
    return np.exp2(-8.0 * np.arange(1, N_SLOTS + 1, dtype=np.float64) / N_SLOTS)


def _prep_body(u_ref, v_ref, ub_ref, vt_ref):
    ub_ref[...] = u_ref[...].astype(BF16)
    vt_ref[...] = v_ref[...].T.astype(BF16)


def _prep_tables(u, v):
    e, d = u.shape
    blk = 512
    return pl.pallas_call(
        _prep_body,
        grid=(e // blk,),
        in_specs=[pl.BlockSpec((blk, d), lambda i: (i, 0)), pl.BlockSpec((blk, d), lambda i: (i, 0))],
        out_specs=[pl.BlockSpec((blk, d), lambda i: (i, 0)), pl.BlockSpec((d, blk), lambda i: (0, i))],
        out_shape=[jax.ShapeDtypeStruct((e, d), BF16), jax.ShapeDtypeStruct((d, e), BF16)],
        compiler_params=_params(("arbitrary",)),
        name="prep_tables",
    )(u, v)


def _proj_core(h, win_ref, woa_ref, conv_fn, c, d, q_ref, kv_refs, gya_ref, sgb_ref):
    def proj(a, n):
        return jnp.dot(h, win_ref[:, a:a + n], preferred_element_type=F32)

    o_q = 3 * c
    o_kv = o_q + N_GROUPS * ATTN_OUT
    o_ga = o_kv + N_GROUPS * 2 * ATTN_OUT
    o_gb = o_ga + d
    u = proj(c, c) * proj(2 * c, c)
    y = conv_fn(u)
    ya = jnp.dot((proj(0, c) * y).astype(BF16), woa_ref[...], preferred_element_type=F32)
    gya_ref[...] = jax.nn.sigmoid(proj(o_ga, d)) * ya
    sgb_ref[...] = jax.nn.sigmoid(proj(o_gb, d))
    if q_ref is not None:
        q_ref[...] = proj(o_q, N_GROUPS * ATTN_OUT) * (HEAD_DIM ** -0.5)
        for g in range(N_GROUPS):
            kv_refs[g][...] = proj(o_kv + g * 2 * ATTN_OUT, 2 * ATTN_OUT)


def _proj_prompt_body(x_ref, g1_ref, win_ref, cw_ref, woa_ref,
                      q_ref, kv0_ref, kv1_ref, kv2_ref, gya_ref, sgb_ref, cst_ref, uext_ref, *, tm, c, d):
    j = pl.program_id(1)
    pre = 8

    @pl.when(j == 0)
    def _():
        uext_ref[0:pre, :] = jnp.zeros((pre, c), F32)

    def conv(u):
        uext_ref[pre:pre + tm, :] = u
        cw = cw_ref[...]
        return (cw[0:1] * uext_ref[pre - 2:pre - 2 + tm, :]
                + cw[1:2] * uext_ref[pre - 1:pre - 1 + tm, :]
                + cw[2:3] * u)

    h = _rms(x_ref[...], g1_ref[...]).astype(BF16)
    _proj_core(h, win_ref, woa_ref, conv, c, d, q_ref, (kv0_ref, kv1_ref, kv2_ref), gya_ref, sgb_ref)
    cst_ref[...] = uext_ref[pre + tm - 2:pre + tm, :]
    uext_ref[0:pre, :] = uext_ref[tm:tm + pre, :]


def _proj_prompt(x, g1, win, cw, woa, tm=256):
    b, s, d = x.shape
    c = cw.shape[1]
    nq = N_GROUPS * ATTN_OUT
    tok = lambda n: pl.BlockSpec((None, tm, n), lambda i, j: (i, j, 0))
    outs = [jax.ShapeDtypeStruct((b, s, nq), F32)]
    outs += [jax.ShapeDtypeStruct((b, s, 2 * ATTN_OUT), F32)] * N_GROUPS
    outs += [jax.ShapeDtypeStruct((b, s, d), F32)] * 2
    outs += [jax.ShapeDtypeStruct((b, CONV_WIDTH - 1, c), F32)]
    return pl.pallas_call(
        functools.partial(_proj_prompt_body, tm=tm, c=c, d=d),
        grid=(b, s // tm),
        in_specs=[tok(d), _resident(), _resident(), _resident(), _resident()],
        out_specs=[tok(nq)] + [tok(2 * ATTN_OUT)] * N_GROUPS + [tok(d), tok(d),
                   pl.BlockSpec((None, CONV_WIDTH - 1, c), lambda i, j: (i, 0, 0))],
        out_shape=outs,
        scratch_shapes=[pltpu.VMEM((tm + 8, c), F32)],
        compiler_params=_params(("arbitrary", "arbitrary")),
        name="proj_prompt",
    )(x, g1, win, cw, woa)


def _proj_sample_body(x_ref, xb_ref, st_ref, g1_ref, win_ref, cw_ref, woa_ref,
                      gya_ref, sgb_ref, cst_ref, cols_ref, uext_ref, *, nb, nt, c, d):
    npre = (CONV_WIDTH - 1) * nb
    o_q = 3 * c
    nqkv = 3 * N_GROUPS * ATTN_OUT
    hb = _rms(xb_ref[...], g1_ref[...]).astype(BF16)
    qkv = jnp.dot(hb, win_ref[:, o_q:o_q + nqkv], preferred_element_type=F32)
    nq = N_GROUPS * ATTN_OUT
    cols_ref[0:nq, :] = (qkv[:, :nq] * (HEAD_DIM ** -0.5)).T
    cols_ref[nq:, :] = qkv[:, nq:].T

    def conv(u):
        uext_ref[0:npre, :] = st_ref[...]
        uext_ref[npre:npre + nt * nb, :] = u
        cw = cw_ref[...]
        return (cw[0:1] * uext_ref[0:nt * nb, :]
                + cw[1:2] * uext_ref[nb:nb + nt * nb, :]
                + cw[2:3] * u)

    h = _rms(x_ref[...], g1_ref[...]).astype(BF16)
    _proj_core(h, win_ref, woa_ref, conv, c, d, None, None, gya_ref, sgb_ref)
    cst_ref[...] = uext_ref[nt * nb:nt * nb + npre, :]


def _proj_sample(x_tm, x_bm, st_tm, g1, win, cw, woa, nb, nt):
    n, d = x_tm.shape
    c = cw.shape[1]
    npre = (CONV_WIDTH - 1) * nb
    outs = [jax.ShapeDtypeStruct((n, d), F32)] * 2
    outs += [jax.ShapeDtypeStruct((npre, c), F32)]
    outs += [jax.ShapeDtypeStruct((3 * N_GROUPS * ATTN_OUT, n), F32)]
    return pl.pallas_call(
        functools.partial(_proj_sample_body, nb=nb, nt=nt, c=c, d=d),
        in_specs=[_resident()] * 7,
        out_specs=[_resident()] * 4,
        out_shape=outs,
        scratch_shapes=[pltpu.VMEM((npre + n, c), F32)],
        compiler_params=pltpu.CompilerParams(vmem_limit_bytes=VMEM_LIMIT),
        name="proj_sample",
    )(x_tm, x_bm, st_tm, g1, win, cw, woa)


def _kvt_body(kv_ref, out_ref):
    out_ref[...] = kv_ref[...].T


def _kv_transposed(kv, keep, tm=256):
    b, s, w = kv.shape
    tm = min(tm, keep)
    off = (s - keep) // tm
    return pl.pallas_call(
        _kvt_body,
        grid=(b, keep // tm),
        in_specs=[pl.BlockSpec((None, tm, w), lambda i, j: (i, j + off, 0))],
        out_specs=pl.BlockSpec((None, w, tm), lambda i, j: (i, 0, j)),
        out_shape=jax.ShapeDtypeStruct((b, w, keep), F32),
        compiler_params=_params(("arbitrary", "arbitrary")),
        name="kv_transposed",
    )(kv)


def _prompt_bias(dil):
    tq = np.arange(WIN)[:, None]
    col = np.arange(2 * WIN)[None, :]
    dist = tq + WIN - col
    ok = (dist >= 0) & (dist <= WIN)
    b = -_slopes()[:, None, None] * (dil * dist)[None]
    return jnp.asarray(np.where(ok[None], b, NEG), F32)


def _attn_prompt_body(*refs, dil, npair, has_prev):
    if has_prev:
        q_ref, k_ref, v_ref, kp_ref, vp_ref, bias_ref, acc_ref, m_ref, l_ref = refs
    else:
        q_ref, k_ref, v_ref, bias_ref, acc_ref, m_ref, l_ref = refs
        kp_ref, vp_ref = k_ref, v_ref
    first = pl.program_id(2) == 0
    pair0 = pl.program_id(1) * npair
    lo = lax.broadcasted_iota(jnp.int32, (WIN, LANES), 1) < HEAD_DIM
    col = lax.broadcasted_iota(jnp.int32, (WIN, 2 * WIN), 1)
    kill = jnp.logical_and(col < WIN, first)
    for p in range(npair):
        ls = slice(p * LANES, (p + 1) * LANES)
        for r in range(dil):
            rows = slice(None) if dil == 1 else pl.ds(r, WIN, stride=dil)
            qp = q_ref[rows, ls]
            kcat = jnp.concatenate([kp_ref[rows, ls], k_ref[rows, ls]], axis=0).astype(BF16)
            vcat = jnp.concatenate([vp_ref[rows, ls], v_ref[rows, ls]], axis=0).astype(BF16)
            res = []
            for hh in range(2):
                qm = jnp.where(lo if hh == 0 else jnp.logical_not(lo), qp, 0.0).astype(BF16)
                s = lax.dot_general(qm, kcat, _NT, preferred_element_type=F32)
                s = jnp.where(kill, NEG, s + bias_ref[2 * (pair0 + p) + hh])
                m = jnp.max(s, axis=-1, keepdims=True)
                e = jnp.exp(s - m)
                l = jnp.sum(e, axis=-1, keepdims=True)
                pv = jnp.dot(e.astype(BF16), vcat, preferred_element_type=F32)
                res.append((m, l, pv))
            acc_ref[rows, ls] = jnp.where(lo, res[0][2], res[1][2])
            m_ref[rows, ls] = jnp.where(lo, res[0][0], res[1][0])
            l_ref[rows, ls] = jnp.where(lo, res[0][1], res[1][1])


def _attn_prompt_group(q, kv, g):
    b, s, nq = q.shape
    dil = GROUPS[g][1]
    span = WIN * dil
    nspan = s // span
    has_prev = nspan > 1
    npair = 4 if dil == 1 else 1
    w = npair * LANES
    nw = ATTN_OUT // w
    cur = lambda off: pl.BlockSpec((None, span, w), lambda i, p, j: (i, j, off + p))
    prev = lambda off: pl.BlockSpec((None, span, w), lambda i, p, j: (i, jnp.maximum(j - 1, 0), off + p))
    in_specs = [cur(g * nw), cur(0), cur(nw)] + ([prev(0), prev(nw)] if has_prev else []) + [_resident()]
    out_sd = jax.ShapeDtypeStruct((b, s, ATTN_OUT), F32)
    acc, m, l = pl.pallas_call(
        functools.partial(_attn_prompt_body, dil=dil, npair=npair, has_prev=has_prev),
        grid=(b, nw, nspan),
        in_specs=in_specs,
        out_specs=[cur(0)] * 3,
        out_shape=[out_sd] * 3,
        compiler_params=_params(("arbitrary", "arbitrary", "arbitrary")),
        name=f"attn_prompt_g{g}",
    )(q, kv, kv, *([kv, kv] if has_prev else []), _prompt_bias(dil))
    shp = (b * s, ATTN_OUT)
    return acc.reshape(shp), m.reshape(shp), l.reshape(shp)


def _sample_bias(nt):
    sl = _slopes()[:, None, None]
    cache, new = [], []
    for win, dil in GROUPS:
        t = np.arange(8)[None, :, None]
        dist = win + t - np.arange(win)[None, None, :]
        ok = (dist % dil == 0) & (dist <= win)
        cache.append(jnp.asarray(np.where(t < nt, np.where(ok, -sl * dist, NEG), 0.0), F32))
        dn = t - np.arange(LANES)[None, None, :]
        okn = (dn >= 0) & (dn % dil == 0)
        new.append(np.where(t < nt, np.where(okn, -sl * dn, NEG), 0.0))
    return cache, jnp.asarray(np.stack(new), F32)


def _sample_body(cols_ref, c0_ref, c1_ref, c2_ref, b0_ref, b1_ref, b2_ref, bn_ref,
                 o0_ref, o1_ref, o2_ref, att_ref, *, nt, hc):
    b = pl.program_id(0)
    col0 = b * nt
    tile = pl.ds(pl.multiple_of((col0 // LANES) * LANES, LANES), LANES)
    to_lane0 = (LANES - col0 % LANES) % LANES
    lane = lax.broadcasted_iota(jnp.int32, (HEAD_DIM, LANES), 1)
    keep = lane < LANES - nt
    c_refs = (c0_ref, c1_ref, c2_ref)
    o_refs = (o0_ref, o1_ref, o2_ref)
    bias_refs = (b0_ref, b1_ref, b2_ref)
    nq = N_GROUPS * ATTN_OUT

    outs = []
    for hl in range(hc):
        h = pl.program_id(1) * hc + hl
        stats = []
        for g in range(N_GROUPS):
            wb = c_refs[g].shape[-1]
            ntile = wb // LANES

            def cols(row0):
                rows = pl.ds(pl.multiple_of(row0 + h * HEAD_DIM, HEAD_DIM), HEAD_DIM)
                return pltpu.roll(cols_ref[rows, tile], to_lane0, axis=1)

            q_t = cols(g * ATTN_OUT)
            q8 = jnp.concatenate([q_t, jnp.zeros_like(q_t)], axis=0).T[0:8, 0:HEAD_DIM].astype(BF16)
            new = (cols(nq + g * 2 * ATTN_OUT), cols(nq + g * 2 * ATTN_OUT + ATTN_OUT))
            old = (c_refs[g][0, hl], c_refs[g][1, hl])
            for kv in range(2):
                rot = [pltpu.roll(old[kv][:, j * LANES:(j + 1) * LANES], LANES - nt, axis=1) for j in range(ntile)]
                rot.append(pltpu.roll(new[kv], LANES - nt, axis=1))
                for j in range(ntile):
                    o_refs[g][kv, hl, :, j * LANES:(j + 1) * LANES] = jnp.where(keep, rot[j], rot[j + 1])
            s = jnp.dot(q8, old[0].astype(BF16), preferred_element_type=F32) + bias_refs[g][h]
            sn = jnp.dot(q8, new[0].astype(BF16), preferred_element_type=F32) + bn_ref[g, h]
            m = jnp.maximum(jnp.max(s, axis=1, keepdims=True), jnp.max(sn, axis=1, keepdims=True))
            e = jnp.exp(s - m)
            en = jnp.exp(sn - m)
            l = jnp.sum(e, axis=1, keepdims=True) + jnp.sum(en, axis=1, keepdims=True)
            acc = (lax.dot_general(e.astype(BF16), old[1].astype(BF16), _NT, preferred_element_type=F32)
                   + lax.dot_general(en.astype(BF16), new[1].astype(BF16), _NT, preferred_element_type=F32))
            stats.append((m, l, acc))
        mm = functools.reduce(jnp.maximum, [st[0] for st in stats])
        num = 0.0
        den = 0.0
        for m, l, acc in stats:
            w = jnp.exp(m - mm)
            num = num + w * acc
            den = den + w * l
        outs.append(num / den)
    att_ref[...] = jnp.concatenate(outs, axis=1)


def _sample_attention(cols, caches_t, nb, nt, hc=4):
    cbias, nbias = _sample_bias(nt)
    for g, (win, _) in enumerate(GROUPS):
        assert caches_t[g].shape[-1] == win, "window buffer must hold a full window"
    assert nt <= 8 and LANES % nt == 0 and (nb * nt) % LANES == 0
    cspec = lambda wb: pl.BlockSpec((None, 2, hc, HEAD_DIM, wb), lambda i, j: (i, 0, j, 0, 0))
    cspecs = [cspec(c.shape[-1]) for c in caches_t]
    res = pl.pallas_call(
        functools.partial(_sample_body, nt=nt, hc=hc),
        grid=(nb, N_SLOTS // hc),
        in_specs=[_resident()] + cspecs + [_resident()] * (N_GROUPS + 1),
        out_specs=cspecs + [pl.BlockSpec((None, 8, hc * HEAD_DIM), lambda i, j: (i, 0, j))],
        out_shape=[jax.ShapeDtypeStruct(c.shape, F32) for c in caches_t]
                  + [jax.ShapeDtypeStruct((nb, 8, ATTN_OUT), F32)],
        compiler_params=_params(("arbitrary", "arbitrary")),
        name="sample_attention",
    )(cols, *caches_t, *cbias, nbias)
    return res[:N_GROUPS], res[N_GROUPS]


def _mix_body(*refs, combine):
    if combine:
        stats = [refs[3 * g:3 * g + 3] for g in range(N_GROUPS)]
        rest = refs[3 * N_GROUPS:]
        mm = functools.reduce(jnp.maximum, [st[1][...] for st in stats])
        num = 0.0
        den = 0.0
        for a_ref, m_ref, l_ref in stats:
            w = jnp.exp(m_ref[...] - mm)
            num = num + w * a_ref[...]
            den = den + w * l_ref[...]
        ob = num / den
    else:
        ob = refs[0][...]
        rest = refs[1:]
    gya_ref, sgb_ref, x_ref, wob_ref, wo_ref, g2_ref, x1_ref, h2_ref, h2t_ref = rest
    yb = jnp.dot(ob.astype(BF16), wob_ref[...], preferred_element_type=F32)
    mix = gya_ref[...] + sgb_ref[...] * yb
    x1 = x_ref[...] + jnp.dot(mix.astype(BF16), wo_ref[...], preferred_element_type=F32)
    x1_ref[...] = x1
    h2 = _rms(x1, g2_ref[...])
    h2_ref[...] = h2.astype(BF16)
    h2t_ref[...] = h2.T.astype(BF16)


def _mix(attn, gya, sgb, x, wob, wo, g2, tm=512):
    n, d = x.shape
    tm = min(tm, n)
    combine = len(attn) > 1
    tok = lambda w: pl.BlockSpec((tm, w), lambda i: (i, 0))
    return pl.pallas_call(
        functools.partial(_mix_body, combine=combine),
        grid=(n // tm,),
        in_specs=[tok(ATTN_OUT)] * len(attn) + [tok(d), tok(d), tok(d), _resident(), _resident(), _resident()],
        out_specs=[tok(d), tok(d), pl.BlockSpec((d, tm), lambda i: (0, i))],
        out_shape=[jax.ShapeDtypeStruct((n, d), F32), jax.ShapeDtypeStruct((n, d), BF16),
                   jax.ShapeDtypeStruct((d, n), BF16)],
        compiler_params=_params(("arbitrary",)),
        name="mix_combine" if combine else "mix",
    )(*attn, gya, sgb, x, wob, wo, g2)


def _sort16_pairs():
    n, pairs, p = PEER_TOPK, [], 1
    while p < n:
        k = p
        while k >= 1:
            for j in range(k % p, n - k, 2 * k):
                for i in range(min(k, n - j - k)):
                    if (i + j) // (2 * p) == (i + j + k) // (2 * p):
                        pairs.append((i + j, i + j + k))
            k //= 2
        p *= 2
    return pairs


def _exchange(x, i, j):
    x[i], x[j] = jnp.maximum(x[i], x[j]), jnp.minimum(x[i], x[j])


def _merge_top16(a, b):
    n = PEER_TOPK
    c = [jnp.maximum(a[i], b[n - 1 - i]) for i in range(n)]
    d = n // 2
    while d >= 1:
        for i in range(n):
            if i & d == 0:
                _exchange(c, i, i + d)
        d //= 2
    return c


def _merge_sublanes(x):
    for shift in (4, 2, 1):
        x = _merge_top16(x, [pltpu.roll(v, shift, axis=0) for v in x])
    return x


def _route_body(h2_ref, wq_ref, keys_ref, s_ref, ec_ref, vals_ref, *, tm, nh):
    q = jnp.dot(h2_ref[...], wq_ref[...], preferred_element_type=F32).astype(BF16)
    for hh in range(2 * nh):
        s_ref[hh] = lax.dot_general(keys_ref[hh], q[:, hh * PEER_NKEYS:(hh + 1) * PEER_NKEYS], _NT,
                                    preferred_element_type=F32)
    nchunk = tm // LANES
    pairs = _sort16_pairs()

    def top_keys(i, carry):
        hh = i // nchunk
        cs = pl.ds(pl.multiple_of((i % nchunk) * LANES, LANES), LANES)
        x = [s_ref[hh, 8 * k:8 * k + 8, cs] for k in range(PEER_NKEYS // 8)]
        for a, b in pairs:
            _exchange(x, a, b)
        x = _merge_sublanes(x)
        for k in range(PEER_TOPK):
            vals_ref[hh, k, :, cs] = x[k]
        return carry

    lax.fori_loop(0, 2 * nh * nchunk, top_keys, 0)

    r8 = lax.broadcasted_iota(jnp.int32, (8, LANES), 0)
    count = functools.reduce(lambda acc, a: jnp.where(r8 == a, PEER_TOPK // (a + 1), acc), range(8), r8 * 0)

    def top_pairs(i, carry):
        h = i // nchunk
        cs = pl.ds(pl.multiple_of((i % nchunk) * LANES, LANES), LANES)
        v1 = [vals_ref[2 * h, k, :, cs] for k in range(PEER_TOPK)]
        v2 = [vals_ref[2 * h + 1, k, :, cs] for k in range(PEER_TOPK)]
        v1col = functools.reduce(lambda acc, a: jnp.where(r8 == a, v1[a], acc), range(1, 8), v1[0])
        x = [jnp.where(count > k, v1col + v2[k], -jnp.inf) for k in range(PEER_TOPK)]
        x = _merge_sublanes(x)
        tail = [v1[8 + k] + v2[0] if k < 8 else jnp.full((8, LANES), -jnp.inf, F32) for k in range(PEER_TOPK)]
        sv = _merge_top16(x, tail)
        z = functools.reduce(lambda a, b: a + b, [jnp.exp(v - sv[0]) for v in sv])[0:1]
        tau = sv[PEER_TOPK - 1]
        ec_ref[2 * h, :, cs] = jnp.exp(s_ref[2 * h, :, cs] - v1[0][0:1]) / z
        ec_ref[2 * h + 1, :, cs] = jnp.exp(s_ref[2 * h + 1, :, cs] - v2[0][0:1])
        for k in range(PEER_NKEYS // 8):
            rows = slice(8 * k, 8 * k + 8)
            s1 = s_ref[2 * h, rows, cs]
            th = jnp.full((8, LANES), jnp.inf, F32)
            for b in range(PEER_TOPK):
                th = jnp.where(s1 + v2[b] >= tau, v2[b], th)
            s_ref[2 * h, rows, cs] = th
        return carry

    lax.fori_loop(0, nh * nchunk, top_pairs, 0)


def _route(h2, wq, keys, tm=512):
    n, d = h2.shape
    nh2 = keys.shape[0]
    tm = min(tm, n)
    big = pl.BlockSpec((nh2, PEER_NKEYS, tm), lambda i: (0, 0, i))
    return pl.pallas_call(
        functools.partial(_route_body, tm=tm, nh=nh2 // 2),
        grid=(n // tm,),
        in_specs=[pl.BlockSpec((tm, d), lambda i: (i, 0)), _resident(), _resident()],
        out_specs=[big, big],
        out_shape=[jax.ShapeDtypeStruct((nh2, PEER_NKEYS, n), F32)] * 2,
        scratch_shapes=[pltpu.VMEM((nh2, PEER_TOPK, 8, tm), F32)],
        compiler_params=_params(("arbitrary",)),
        name="peer_route",
    )(h2, wq, keys)


def _gate_half(grp, col0, act_ref, p_ref, s_ref, ec_ref, *, half, eblk, nh):
    rows = PEER_NKEYS // 2
    for sub in range(eblk // PEER_NKEYS):
        for c0 in range(0, half, LANES):
            cs = slice(col0 + c0, col0 + c0 + LANES)
            for k0 in range(0, PEER_NKEYS, rows):
                ks = slice(k0, k0 + rows)
                g = jnp.zeros((rows, LANES), F32)
                for h in range(nh):
                    th = s_ref[2 * h, grp, cs][sub:sub + 1]
                    c1 = ec_ref[2 * h, grp, cs][sub:sub + 1]
                    w = ec_ref[2 * h + 1, ks, cs] * c1
                    g = g + jnp.where(s_ref[2 * h + 1, ks, cs] >= th, w, 0.0)
                rs = slice(sub * PEER_NKEYS + k0, sub * PEER_NKEYS + k0 + rows)
                a = act_ref[rs, c0:c0 + LANES]
                gelu = 0.5 * a * (1.0 + lax.erf(a * np.float32(np.sqrt(0.5))))
                p_ref[rs, c0:c0 + LANES] = (g * gelu).astype(BF16)


def _experts_body(h2t_ref, u_ref, vt_ref, s_ref, ec_ref, x1_ref, gf_ref, y_ref,
                  acc_ref, act0_ref, act1_ref, p0_ref, p1_ref, *, tm, eblk, nh):
    j = pl.program_id(1)
    half = tm // 2
    halves = ((act0_ref, p0_ref, 0), (act1_ref, p1_ref, half))
    grp = pl.ds(pl.multiple_of(j * 8, 8), 8)

    @pl.when(j == 0)
    def _():
        acc_ref[...] = jnp.zeros_like(acc_ref)

    for act_ref, _, off in halves:
        act_ref[...] = jnp.dot(u_ref[...], h2t_ref[:, off:off + half], preferred_element_type=F32)
    for act_ref, p_ref, off in halves:
        _gate_half(grp, off, act_ref, p_ref, s_ref, ec_ref, half=half, eblk=eblk, nh=nh)
    for _, p_ref, off in halves:
        acc_ref[:, off:off + half] += jnp.dot(vt_ref[...], p_ref[...], preferred_element_type=F32)

    @pl.when(j == pl.num_programs(1) - 1)
    def _():
        y_ref[...] = _rms(x1_ref[...] + acc_ref[...].T, gf_ref[...])


def _experts(h2t, ub, vtb, s_all, ec_all, x1, gf, tm=512):
    d, n = h2t.shape
    e = ub.shape[0]
    nh2 = s_all.shape[0]
    tm = min(tm, n)
    eblk = 8 * PEER_NKEYS
    big = pl.BlockSpec((nh2, PEER_NKEYS, tm), lambda i, j: (0, 0, i))
    return pl.pallas_call(
        functools.partial(_experts_body, tm=tm, eblk=eblk, nh=nh2 // 2),
        grid=(n // tm, e // eblk),
        in_specs=[
            pl.BlockSpec((d, tm), lambda i, j: (0, i)),
            pl.BlockSpec((eblk, d), lambda i, j: (j, 0)),
            pl.BlockSpec((d, eblk), lambda i, j: (0, j)),
            big, big,
            pl.BlockSpec((tm, d), lambda i, j: (i, 0)),
            _resident(),
        ],
        out_specs=pl.BlockSpec((tm, d), lambda i, j: (i, 0)),
        out_shape=jax.ShapeDtypeStruct((n, d), F32),
        scratch_shapes=[pltpu.VMEM((d, tm), F32)] + [pltpu.VMEM((eblk, tm // 2), F32)] * 2
                       + [pltpu.VMEM((eblk, tm // 2), BF16)] * 2,
        compiler_params=_params(("arbitrary", "arbitrary")),
        name="peer_experts",
    )(h2t, ub, vtb, s_all, ec_all, x1, gf)


def _arrange_w_in(w, c, d):
    na = N_GROUPS * ATTN_OUT
    o = 3 * c
    k = w[:, o + na:o + 2 * na]
    v = w[:, o + 2 * na:o + 3 * na]
    kv = []
    for g in range(N_GROUPS):
        kv += [k[:, g * ATTN_OUT:(g + 1) * ATTN_OUT], v[:, g * ATTN_OUT:(g + 1) * ATTN_OUT]]
    return jnp.concatenate([w[:, :o + na]] + kv + [w[:, o + 3 * na:]], axis=1).astype(BF16)


def _tail(attn, gya, sgb, x, wts):
    x1, h2, h2t = _mix(attn, gya, sgb, x, wts["wob"], wts["wo"], wts["g2"])
    s_all, ec_all = _route(h2, wts["wq"], wts["keys"])
    return _experts(h2t, wts["ub"], wts["vtb"], s_all, ec_all, x1, wts["gf"])


def kernel(x_prompt, x_sample, cache_kv_w128, cache_kv_w512, cache_kv_w2048, state_conv, norm1_g, w_in,
           conv_w, w_out_a, w_out_b, w_o, norm2_g, peer_wq, peer_keys, peer_u, peer_v, final_g):
    depth = w_in.shape[0]
    assert depth == 1, "single-layer trunk"
    b, s, d = x_prompt.shape
    nb, nt, _ = x_sample.shape
    c = conv_w.shape[-1]
    nh = peer_keys.shape[2]

    ub, vtb = _prep_tables(peer_u[0], peer_v[0])
    wts = dict(
        wob=w_out_b[0].astype(BF16), wo=w_o[0].astype(BF16), g2=norm2_g[0][None, :],
        wq=peer_wq[0].astype(BF16),
        keys=jnp.swapaxes(peer_keys[0], 0, 1).reshape(2 * nh, PEER_NKEYS, -1).astype(BF16),
        ub=ub, vtb=vtb, gf=final_g[None, :],
    )
    win = _arrange_w_in(w_in[0], c, d)
    woa = w_out_a[0].astype(BF16)
    g1 = norm1_g[0][None, :]
    cw = conv_w[0]

    q, kv0, kv1, kv2, gya, sgb, cst_p = _proj_prompt(x_prompt, g1, win, cw, woa)
    kvs = (kv0, kv1, kv2)
    attn = []
    for g in range(N_GROUPS):
        attn += list(_attn_prompt_group(q, kvs[g], g))
    n = b * s
    y_prompt = _tail(attn, gya.reshape(n, d), sgb.reshape(n, d), x_prompt.reshape(n, d), wts).reshape(b, s, d)
    to_rows_major = lambda a: jnp.transpose(a, (0, 4, 1, 2, 3))[None]
    kv_prompt = []
    for g, (win_g, _) in enumerate(GROUPS):
        keep = min(win_g, s)
        kt = _kv_transposed(kvs[g], keep)
        kv_prompt.append(to_rows_major(kt.reshape(b, 2, N_SLOTS, HEAD_DIM, keep)))

    x_tm = jnp.swapaxes(x_sample, 0, 1).reshape(nt * nb, d)
    st_tm = jnp.swapaxes(state_conv[0], 0, 1).reshape((CONV_WIDTH - 1) * nb, c)
    gya_s, sgb_s, cst_s, cols = _proj_sample(x_tm, x_sample.reshape(nb * nt, d), st_tm, g1, win, cw, woa, nb, nt)
    caches_t = [jnp.transpose(cv[0], (0, 2, 3, 4, 1)) for cv in (cache_kv_w128, cache_kv_w512, cache_kv_w2048)]
    new_caches, att = _sample_attention(cols, caches_t, nb, nt)
    ob_s = jnp.swapaxes(att[:, :nt, :], 0, 1).reshape(nt * nb, ATTN_OUT)
    y_s = _tail([ob_s], gya_s, sgb_s, x_tm, wts)
    y_sample = jnp.swapaxes(y_s.reshape(nt, nb, d), 0, 1)
    kv_sample = [to_rows_major(o) for o in new_caches]
    conv_sample = jnp.swapaxes(cst_s.reshape(CONV_WIDTH - 1, nb, c), 0, 1)[None]

    return (y_prompt, y_sample, kv_prompt[0], kv_prompt[1], kv_prompt[2], cst_p[None],
            kv_sample[0], kv_sample[1], kv_sample[2], conv_sample)
```

```python
import functools

import jax
import jax.numpy as jnp
import numpy as np
from jax import lax
from jax.experimental import pallas as pl
from jax.experimental.pallas import tpu as pltpu

F32 = jnp.float32
BF16 = jnp.bfloat16

HEAD_DIM = 64
N_SLOTS = 8
GROUPS = ((128, 1), (512, 4), (2048, 16))
N_GROUPS = len(GROUPS)
ATTN_OUT = N_SLOTS * HEAD_DIM
WIN = 128
CONV_WIDTH = 3
PEER_TOPK = 16
PEER_NKEYS = 128
EPS = 1e-6
NEG = -1e30

LANES = 128
VMEM_LIMIT = 56 * 1024 * 1024

_NT = (((1,), (1,)), ((), ()))


def _params(sem):
    return pltpu.CompilerParams(dimension_semantics=sem, vmem_limit_bytes=VMEM_LIMIT)


def _resident():
    return pl.BlockSpec(memory_space=pltpu.VMEM)


def _rms(x, g):
    return x * lax.rsqrt(jnp.mean(x * x, axis=-1, keepdims=True) + EPS) * g


def _slopes():
    return np.exp2(-8.0 * np.arange(1, N_SLOTS + 1, dtype=np.float64) / N_SLOTS)


def _prep_body(u_ref, v_ref, ub_ref, vt_ref):
    ub_ref[...] = u_ref[...].astype(BF16)
    vt_ref[...] = v_ref[...].T.astype(BF16)


def _prep_tables(u, v):
    e, d = u.shape
    blk = 512
    return pl.pallas_call(
        _prep_body,
        grid=(e // blk,),
        in_specs=[pl.BlockSpec((blk, d), lambda i: (i, 0)), pl.BlockSpec((blk, d), lambda i: (i, 0))],
        out_specs=[pl.BlockSpec((blk, d), lambda i: (i, 0)), pl.BlockSpec((d, blk), lambda i: (0, i))],
        out_shape=[jax.ShapeDtypeStruct((e, d), BF16), jax.ShapeDtypeStruct((d, e), BF16)],
        compiler_params=_params(("arbitrary",)),
        name="prep_tables",
    )(u, v)


def _proj_core(h, win_ref, woa_ref, conv_fn, c, d, q_ref, kv_refs, gya_ref, sgb_ref):
    def proj(a, n):
        return jnp.dot(h, win_ref[:, a:a + n], preferred_element_type=F32)

    o_q = 3 * c
    o_kv = o_q + N_GROUPS * ATTN_OUT
    o_ga = o_kv + N_GROUPS * 2 * ATTN_OUT
    o_gb = o_ga + d
    u = proj(c, c) * proj(2 * c, c)
    y = conv_fn(u)
    ya = jnp.dot((proj(0, c) * y).astype(BF16), woa_ref[...], preferred_element_type=F32)
    gya_ref[...] = jax.nn.sigmoid(proj(o_ga, d)) * ya
    sgb_ref[...] = jax.nn.sigmoid(proj(o_gb, d))
    if q_ref is not None:
        q_ref[...] = proj(o_q, N_GROUPS * ATTN_OUT) * (HEAD_DIM ** -0.5)
        for g in range(N_GROUPS):
            kv_refs[g][...] = proj(o_kv + g * 2 * ATTN_OUT, 2 * ATTN_OUT)


def _proj_prompt_body(x_ref, g1_ref, win_ref, cw_ref, woa_ref,
                      q_ref, kv0_ref, kv1_ref, kv2_ref, gya_ref, sgb_ref, cst_ref, uext_ref, *, tm, c, d):
    j = pl.program_id(1)
    pre = 8

    @pl.when(j == 0)
    def _():
        uext_ref[0:pre, :] = jnp.zeros((pre, c), F32)

    def conv(u):
        uext_ref[pre:pre + tm, :] = u
        cw = cw_ref[...]
        return (cw[0:1] * uext_ref[pre - 2:pre - 2 + tm, :]
                + cw[1:2] * uext_ref[pre - 1:pre - 1 + tm, :]
                + cw[2:3] * u)

    h = _rms(x_ref[...], g1_ref[...]).astype(BF16)
    _proj_core(h, win_ref, woa_ref, conv, c, d, q_ref, (kv0_ref, kv1_ref, kv2_ref), gya_ref, sgb_ref)
    cst_ref[...] = uext_ref[pre + tm - 2:pre + tm, :]
    uext_ref[0:pre, :] = uext_ref[tm:tm + pre, :]


def _proj_prompt(x, g1, win, cw, woa, tm=256):
    b, s, d = x.shape
    c = cw.shape[1]
    nq = N_GROUPS * ATTN_OUT
    tok = lambda n: pl.BlockSpec((None, tm, n), lambda i, j: (i, j, 0))
    outs = [jax.ShapeDtypeStruct((b, s, nq), F32)]
    outs += [jax.ShapeDtypeStruct((b, s, 2 * ATTN_OUT), F32)] * N_GROUPS
    outs += [jax.ShapeDtypeStruct((b, s, d), F32)] * 2
    outs += [jax.ShapeDtypeStruct((b, CONV_WIDTH - 1, c), F32)]
    return pl.pallas_call(
        functools.partial(_proj_prompt_body, tm=tm, c=c, d=d),
        grid=(b, s // tm),
        in_specs=[tok(d), _resident(), _resident(), _resident(), _resident()],
        out_specs=[tok(nq)] + [tok(2 * ATTN_OUT)] * N_GROUPS + [tok(d), tok(d),
                   pl.BlockSpec((None, CONV_WIDTH - 1, c), lambda i, j: (i, 0, 0))],
        out_shape=outs,
        scratch_shapes=[pltpu.VMEM((tm + 8, c), F32)],
        compiler_params=_params(("arbitrary", "arbitrary")),
        name="proj_prompt",
    )(x, g1, win, cw, woa)


def _proj_sample_body(x_ref, xb_ref, st_ref, g1_ref, win_ref, cw_ref, woa_ref,
                      gya_ref, sgb_ref, cst_ref, cols_ref, uext_ref, *, nb, nt, c, d):
    npre = (CONV_WIDTH - 1) * nb
    o_q = 3 * c
    nqkv = 3 * N_GROUPS * ATTN_OUT
    hb = _rms(xb_ref[...], g1_ref[...]).astype(BF16)
    qkv = jnp.dot(hb, win_ref[:, o_q:o_q + nqkv], preferred_element_type=F32)
    nq = N_GROUPS * ATTN_OUT
    cols_ref[0:nq, :] = (qkv[:, :nq] * (HEAD_DIM ** -0.5)).T
    cols_ref[nq:, :] = qkv[:, nq:].T

    def conv(u):
        uext_ref[0:npre, :] = st_ref[...]
        uext_ref[npre:npre + nt * nb, :] = u
        cw = cw_ref[...]
        return (cw[0:1] * uext_ref[0:nt * nb, :]
                + cw[1:2] * uext_ref[nb:nb + nt * nb, :]
                + cw[2:3] * u)

    h = _rms(x_ref[...], g1_ref[...]).astype(BF16)
    _proj_core(h, win_ref, woa_ref, conv, c, d, None, None, gya_ref, sgb_ref)
    cst_ref[...] = uext_ref[nt * nb:nt * nb + npre, :]


def _proj_sample(x_tm, x_bm, st_tm, g1, win, cw, woa, nb, nt):
    n, d = x_tm.shape
    c = cw.shape[1]
    npre = (CONV_WIDTH - 1) * nb
    outs = [jax.ShapeDtypeStruct((n, d), F32)] * 2
    outs += [jax.ShapeDtypeStruct((npre, c), F32)]
    outs += [jax.ShapeDtypeStruct((3 * N_GROUPS * ATTN_OUT, n), F32)]
    return pl.pallas_call(
        functools.partial(_proj_sample_body, nb=nb, nt=nt, c=c, d=d),
        in_specs=[_resident()] * 7,
        out_specs=[_resident()] * 4,
        out_shape=outs,
        scratch_shapes=[pltpu.VMEM((npre + n, c), F32)],
        compiler_params=pltpu.CompilerParams(vmem_limit_bytes=VMEM_LIMIT),
        name="proj_sample",
    )(x_tm, x_bm, st_tm, g1, win, cw, woa)


def _kvt_body(kv_ref, out_ref):
    out_ref[...] = kv_ref[...].T


def _kv_transposed(kv, keep, tm=256):
    b, s, w = kv.shape
    tm = min(tm, keep)
    off = (s - keep) // tm
    return pl.pallas_call(
        _kvt_body,
        grid=(b, keep // tm),
        in_specs=[pl.BlockSpec((None, tm, w), lambda i, j: (i, j + off, 0))],
        out_specs=pl.BlockSpec((None, w, tm), lambda i, j: (i, 0, j)),
        out_shape=jax.ShapeDtypeStruct((b, w, keep), F32),
        compiler_params=_params(("arbitrary", "arbitrary")),
        name="kv_transposed",
    )(kv)


def _prompt_bias(dil):
    tq = np.arange(WIN)[:, None]
    col = np.arange(2 * WIN)[None, :]
    dist = tq + WIN - col
    ok = (dist >= 0) & (dist <= WIN)
    b = -_slopes()[:, None, None] * (dil * dist)[None]
    return jnp.asarray(np.where(ok[None], b, NEG), F32)


def _attn_prompt_body(*refs, dil, npair, has_prev):
    if has_prev:
        q_ref, k_ref, v_ref, kp_ref, vp_ref, bias_ref, acc_ref, m_ref, l_ref = refs
    else:
        q_ref, k_ref, v_ref, bias_ref, acc_ref, m_ref, l_ref = refs
        kp_ref, vp_ref = k_ref, v_ref
    first = pl.program_id(2) == 0
    pair0 = pl.program_id(1) * npair
    lo = lax.broadcasted_iota(jnp.int32, (WIN, LANES), 1) < HEAD_DIM
    col = lax.broadcasted_iota(jnp.int32, (WIN, 2 * WIN), 1)
    kill = jnp.logical_and(col < WIN, first)
    for p in range(npair):
        ls = slice(p * LANES, (p + 1) * LANES)
        for r in range(dil):
            rows = slice(None) if dil == 1 else pl.ds(r, WIN, stride=dil)
            qp = q_ref[rows, ls]
            kcat = jnp.concatenate([kp_ref[rows, ls], k_ref[rows, ls]], axis=0).astype(BF16)
            vcat = jnp.concatenate([vp_ref[rows, ls], v_ref[rows, ls]], axis=0).astype(BF16)
            res = []
            for hh in range(2):
                qm = jnp.where(lo if hh == 0 else jnp.logical_not(lo), qp, 0.0).astype(BF16)
                s = lax.dot_general(qm, kcat, _NT, preferred_element_type=F32)
                s = jnp.where(kill, NEG, s + bias_ref[2 * (pair0 + p) + hh])
                m = jnp.max(s, axis=-1, keepdims=True)
                e = jnp.exp(s - m)
                l = jnp.sum(e, axis=-1, keepdims=True)
                pv = jnp.dot(e.astype(BF16), vcat, preferred_element_type=F32)
                res.append((m, l, pv))
            acc_ref[rows, ls] = jnp.where(lo, res[0][2], res[1][2])
            m_ref[rows, ls] = jnp.where(lo, res[0][0], res[1][0])
            l_ref[rows, ls] = jnp.where(lo, res[0][1], res[1][1])


def _attn_prompt_group(q, kv, g):
    b, s, nq = q.shape
    dil = GROUPS[g][1]
    span = WIN * dil
    nspan = s // span
    has_prev = nspan > 1
    npair = 4 if dil == 1 else 1
    w = npair * LANES
    nw = ATTN_OUT // w
    cur = lambda off: pl.BlockSpec((None, span, w), lambda i, p, j: (i, j, off + p))
    prev = lambda off: pl.BlockSpec((None, span, w), lambda i, p, j: (i, jnp.maximum(j - 1, 0), off + p))
    in_specs = [cur(g * nw), cur(0), cur(nw)] + ([prev(0), prev(nw)] if has_prev else []) + [_resident()]
    out_sd = jax.ShapeDtypeStruct((b, s, ATTN_OUT), F32)
    acc, m, l = pl.pallas_call(
        functools.partial(_attn_prompt_body, dil=dil, npair=npair, has_prev=has_prev),
        grid=(b, nw, nspan),
        in_specs=in_specs,
        out_specs=[cur(0)] * 3,
        out_shape=[out_sd] * 3,
        compiler_params=_params(("arbitrary", "arbitrary", "arbitrary")),
        name=f"attn_prompt_g{g}",
    )(q, kv, kv, *([kv, kv] if has_prev else []), _prompt_bias(dil))
    shp = (b * s, ATTN_OUT)
    return acc.reshape(shp), m.reshape(shp), l.reshape(shp)


def _sample_bias(nt):
    sl = _slopes()[:, None, None]
    cache, new = [], []
    for win, dil in GROUPS:
        t = np.arange(8)[None, :, None]
        dist = win + t - np.arange(win)[None, None, :]
        ok = (dist % dil == 0) & (dist <= win)
        cache.append(jnp.asarray(np.where(t < nt, np.where(ok, -sl * dist, NEG), 0.0), F32))
        dn = t - np.arange(LANES)[None, None, :]
        okn = (dn >= 0) & (dn % dil == 0)
        new.append(np.where(t < nt, np.where(okn, -sl * dn, NEG), 0.0))
    return cache, jnp.asarray(np.stack(new), F32)


def _sample_body(cols_ref, c0_ref, c1_ref, c2_ref, b0_ref, b1_ref, b2_ref, bn_ref,
                 o0_ref, o1_ref, o2_ref, att_ref, *, nt, hc):
    b = pl.program_id(0)
    col0 = b * nt
    tile = pl.ds(pl.multiple_of((col0 // LANES) * LANES, LANES), LANES)
    to_lane0 = (LANES - col0 % LANES) % LANES
    lane = lax.broadcasted_iota(jnp.int32, (HEAD_DIM, LANES), 1)
    keep = lane < LANES - nt
    c_refs = (c0_ref, c1_ref, c2_ref)
    o_refs = (o0_ref, o1_ref, o2_ref)
    bias_refs = (b0_ref, b1_ref, b2_ref)
    nq = N_GROUPS * ATTN_OUT

    outs = []
    for hl in range(hc):
        h = pl.program_id(1) * hc + hl
        stats = []
        for g in range(N_GROUPS):
            wb = c_refs[g].shape[-1]
            ntile = wb // LANES

            def cols(row0):
                rows = pl.ds(pl.multiple_of(row0 + h * HEAD_DIM, HEAD_DIM), HEAD_DIM)
                return pltpu.roll(cols_ref[rows, tile], to_lane0, axis=1)

            q_t = cols(g * ATTN_OUT)
            q8 = jnp.concatenate([q_t, jnp.zeros_like(q_t)], axis=0).T[0:8, 0:HEAD_DIM].astype(BF16)
            new = (cols(nq + g * 2 * ATTN_OUT), cols(nq + g * 2 * ATTN_OUT + ATTN_OUT))
            old = (c_refs[g][0, hl], c_refs[g][1, hl])
            for kv in range(2):
                rot = [pltpu.roll(old[kv][:, j * LANES:(j + 1) * LANES], LANES - nt, axis=1) for j in range(ntile)]
                rot.append(pltpu.roll(new[kv], LANES - nt, axis=1))
                for j in range(ntile):
                    o_refs[g][kv, hl, :, j * LANES:(j + 1) * LANES] = jnp.where(keep, rot[j], rot[j + 1])
            s = jnp.dot(q8, old[0].astype(BF16), preferred_element_type=F32) + bias_refs[g][h]
            sn = jnp.dot(q8, new[0].astype(BF16), preferred_element_type=F32) + bn_ref[g, h]
            m = jnp.maximum(jnp.max(s, axis=1, keepdims=True), jnp.max(sn, axis=1, keepdims=True))
            e = jnp.exp(s - m)
            en = jnp.exp(sn - m)
            l = jnp.sum(e, axis=1, keepdims=True) + jnp.sum(en, axis=1, keepdims=True)
            acc = (lax.dot_general(e.astype(BF16), old[1].astype(BF16), _NT, preferred_element_type=F32)
                   + lax.dot_general(en.astype(BF16), new[1].astype(BF16), _NT, preferred_element_type=F32))
            stats.append((m, l, acc))
        mm = functools.reduce(jnp.maximum, [st[0] for st in stats])
        num = 0.0
        den = 0.0
        for m, l, acc in stats:
            w = jnp.exp(m - mm)
            num = num + w * acc
            den = den + w * l
        outs.append(num / den)
    att_ref[...] = jnp.concatenate(outs, axis=1)


def _sample_attention(cols, caches_t, nb, nt, hc=4):
    cbias, nbias = _sample_bias(nt)
    for g, (win, _) in enumerate(GROUPS):
        assert caches_t[g].shape[-1] == win, "window buffer must hold a full window"
    assert nt <= 8 and LANES % nt == 0 and (nb * nt) % LANES == 0
    cspec = lambda wb: pl.BlockSpec((None, 2, hc, HEAD_DIM, wb), lambda i, j: (i, 0, j, 0, 0))
    cspecs = [cspec(c.shape[-1]) for c in caches_t]
    res = pl.pallas_call(
        functools.partial(_sample_body, nt=nt, hc=hc),
        grid=(nb, N_SLOTS // hc),
        in_specs=[_resident()] + cspecs + [_resident()] * (N_GROUPS + 1),
        out_specs=cspecs + [pl.BlockSpec((None, 8, hc * HEAD_DIM), lambda i, j: (i, 0, j))],
        out_shape=[jax.ShapeDtypeStruct(c.shape, F32) for c in caches_t]
                  + [jax.ShapeDtypeStruct((nb, 8, ATTN_OUT), F32)],
        compiler_params=_params(("arbitrary", "arbitrary")),
        name="sample_attention",
    )(cols, *caches_t, *cbias, nbias)
    return res[:N_GROUPS], res[N_GROUPS]


def _mix_body(*refs, combine):
    if combine:
        stats = [refs[3 * g:3 * g + 3] for g in range(N_GROUPS)]
        rest = refs[3 * N_GROUPS:]
        mm = functools.reduce(jnp.maximum, [st[1][...] for st in stats])
        num = 0.0
        den = 0.0
        for a_ref, m_ref, l_ref in stats:
            w = jnp.exp(m_ref[...] - mm)
            num = num + w * a_ref[...]
            den = den + w * l_ref[...]
        ob = num / den
    else:
        ob = refs[0][...]
        rest = refs[1:]
    gya_ref, sgb_ref, x_ref, wob_ref, wo_ref, g2_ref, x1_ref, h2_ref, h2t_ref = rest
    yb = jnp.dot(ob.astype(BF16), wob_ref[...], preferred_element_type=F32)
    mix = gya_ref[...] + sgb_ref[...] * yb
    x1 = x_ref[...] + jnp.dot(mix.astype(BF16), wo_ref[...], preferred_element_type=F32)
    x1_ref[...] = x1
    h2 = _rms(x1, g2_ref[...])
    h2_ref[...] = h2.astype(BF16)
    h2t_ref[...] = h2.T.astype(BF16)


def _mix(attn, gya, sgb, x, wob, wo, g2, tm=512):
    n, d = x.shape
    tm = min(tm, n)
    combine = len(attn) > 1
    tok = lambda w: pl.BlockSpec((tm, w), lambda i: (i, 0))
    return pl.pallas_call(
        functools.partial(_mix_body, combine=combine),
        grid=(n // tm,),
        in_specs=[tok(ATTN_OUT)] * len(attn) + [tok(d), tok(d), tok(d), _resident(), _resident(), _resident()],
        out_specs=[tok(d), tok(d), pl.BlockSpec((d, tm), lambda i: (0, i))],
        out_shape=[jax.ShapeDtypeStruct((n, d), F32), jax.ShapeDtypeStruct((n, d), BF16),
                   jax.ShapeDtypeStruct((d, n), BF16)],
        compiler_params=_params(("arbitrary",)),
        name="mix_combine" if combine else "mix",
    )(*attn, gya, sgb, x, wob, wo, g2)


def _sort16_pairs():
    n, pairs, p = PEER_TOPK, [], 1
    while p < n:
        k = p
        while k >= 1:
            for j in range(k % p, n - k, 2 * k):
                for i in range(min(k, n - j - k)):
                    if (i + j) // (2 * p) == (i + j + k) // (2 * p):
                        pairs.append((i + j, i + j + k))
            k //= 2
        p *= 2
    return pairs


def _exchange(x, i, j):
    x[i], x[j] = jnp.maximum(x[i], x[j]), jnp.minimum(x[i], x[j])


def _merge_top16(a, b):
    n = PEER_TOPK
    c = [jnp.maximum(a[i], b[n - 1 - i]) for i in range(n)]
    d = n // 2
    while d >= 1:
        for i in range(n):
            if i & d == 0:
                _exchange(c, i, i + d)
        d //= 2
    return c


def _merge_sublanes(x):
    for shift in (4, 2, 1):
        x = _merge_top16(x, [pltpu.roll(v, shift, axis=0) for v in x])
    return x


def _route_body(h2_ref, wq_ref, keys_ref, s_ref, ec_ref, vals_ref, *, tm, nh):
    q = jnp.dot(h2_ref[...], wq_ref[...], preferred_element_type=F32).astype(BF16)
    for hh in range(2 * nh):
        s_ref[hh] = lax.dot_general(keys_ref[hh], q[:, hh * PEER_NKEYS:(hh + 1) * PEER_NKEYS], _NT,
                                    preferred_element_type=F32)
    nchunk = tm // LANES
    pairs = _sort16_pairs()

    def top_keys(i, carry):
        hh = i // nchunk
        cs = pl.ds(pl.multiple_of((i % nchunk) * LANES, LANES), LANES)
        x = [s_ref[hh, 8 * k:8 * k + 8, cs] for k in range(PEER_NKEYS // 8)]
        for a, b in pairs:
            _exchange(x, a, b)
        x = _merge_sublanes(x)
        for k in range(PEER_TOPK):
            vals_ref[hh, k, :, cs] = x[k]
        return carry

    lax.fori_loop(0, 2 * nh * nchunk, top_keys, 0)

    r8 = lax.broadcasted_iota(jnp.int32, (8, LANES), 0)
    count = functools.reduce(lambda acc, a: jnp.where(r8 == a, PEER_TOPK // (a + 1), acc), range(8), r8 * 0)

    def top_pairs(i, carry):
        h = i // nchunk
        cs = pl.ds(pl.multiple_of((i % nchunk) * LANES, LANES), LANES)
        v1 = [vals_ref[2 * h, k, :, cs] for k in range(PEER_TOPK)]
        v2 = [vals_ref[2 * h + 1, k, :, cs] for k in range(PEER_TOPK)]
        v1col = functools.reduce(lambda acc, a: jnp.where(r8 == a, v1[a], acc), range(1, 8), v1[0])
        x = [jnp.where(count > k, v1col + v2[k], -jnp.inf) for k in range(PEER_TOPK)]
        x = _merge_sublanes(x)
        tail = [v1[8 + k] + v2[0] if k < 8 else jnp.full((8, LANES), -jnp.inf, F32) for k in range(PEER_TOPK)]
        sv = _merge_top16(x, tail)
        z = functools.reduce(lambda a, b: a + b, [jnp.exp(v - sv[0]) for v in sv])[0:1]
        tau = sv[PEER_TOPK - 1]
        ec_ref[2 * h, :, cs] = jnp.exp(s_ref[2 * h, :, cs] - v1[0][0:1]) / z
        ec_ref[2 * h + 1, :, cs] = jnp.exp(s_ref[2 * h + 1, :, cs] - v2[0][0:1])
        for k in range(PEER_NKEYS // 8):
            rows = slice(8 * k, 8 * k + 8)
            s1 = s_ref[2 * h, rows, cs]
            th = jnp.full((8, LANES), jnp.inf, F32)
            for b in range(PEER_TOPK):
                th = jnp.where(s1 + v2[b] >= tau, v2[b], th)
            s_ref[2 * h, rows, cs] = th
        return carry

    lax.fori_loop(0, nh * nchunk, top_pairs, 0)


def _route(h2, wq, keys, tm=512):
    n, d = h2.shape
    nh2 = keys.shape[0]
    tm = min(tm, n)
    big = pl.BlockSpec((nh2, PEER_NKEYS, tm), lambda i: (0, 0, i))
    return pl.pallas_call(
        functools.partial(_route_body, tm=tm, nh=nh2 // 2),
        grid=(n // tm,),
        in_specs=[pl.BlockSpec((tm, d), lambda i: (i, 0)), _resident(), _resident()],
        out_specs=[big, big],
        out_shape=[jax.ShapeDtypeStruct((nh2, PEER_NKEYS, n), F32)] * 2,
        scratch_shapes=[pltpu.VMEM((nh2, PEER_TOPK, 8, tm), F32)],
        compiler_params=_params(("arbitrary",)),
        name="peer_route",
    )(h2, wq, keys)


def _gate_block(grp, row0, act_ref, p_ref, s_ref, ec_ref, *, tm, eblk, nh):
    rows = PEER_NKEYS // 2
    for sub in range(eblk // PEER_NKEYS):
        r = row0 + sub
        for c0 in range(0, tm, LANES):
            cs = slice(c0, c0 + LANES)
            for k0 in range(0, PEER_NKEYS, rows):
                ks = slice(k0, k0 + rows)
                g = jnp.zeros((rows, LANES), F32)
                for h in range(nh):
                    th = s_ref[2 * h, grp, cs][r:r + 1]
                    c1 = ec_ref[2 * h, grp, cs][r:r + 1]
                    w = ec_ref[2 * h + 1, ks, cs] * c1
                    g = g + jnp.where(s_ref[2 * h + 1, ks, cs] >= th, w, 0.0)
                rs = slice(sub * PEER_NKEYS + k0, sub * PEER_NKEYS + k0 + rows)
                a = act_ref[rs, cs]
                gelu = 0.5 * a * (1.0 + lax.erf(a * np.float32(np.sqrt(0.5))))
                p_ref[rs, cs] = (g * gelu).astype(BF16)


def _experts_body(h2t_ref, u0_ref, uo_ref, ue_ref, vt_ref, s_ref, ec_ref, x1_ref, gf_ref, y_ref,
                  acc_ref, acte_ref, acto_ref, pe_ref, po_ref, *, tm, eblk, nh):
    j = pl.program_id(1)
    grp = pl.ds(pl.multiple_of(j * 8, 8), 8)
    cur = j % 2
    gate = functools.partial(_gate_block, s_ref=s_ref, ec_ref=ec_ref, tm=tm, eblk=eblk, nh=nh)

    @pl.when(j == 0)
    def _():
        acc_ref[...] = jnp.zeros_like(acc_ref)
        acte_ref[0] = jnp.dot(u0_ref[...], h2t_ref[...], preferred_element_type=F32)

    acto_ref[...] = jnp.dot(uo_ref[...], h2t_ref[...], preferred_element_type=F32)
    gate(grp, 0, acte_ref.at[cur], pe_ref)
    acc_ref[...] += jnp.dot(vt_ref[:, 0:eblk], pe_ref[...], preferred_element_type=F32)
    acte_ref[1 - cur] = jnp.dot(ue_ref[...], h2t_ref[...], preferred_element_type=F32)
    gate(grp, eblk // PEER_NKEYS, acto_ref, po_ref)
    acc_ref[...] += jnp.dot(vt_ref[:, eblk:2 * eblk], po_ref[...], preferred_element_type=F32)

    @pl.when(j == pl.num_programs(1) - 1)
    def _():
        y_ref[...] = _rms(x1_ref[...] + acc_ref[...].T, gf_ref[...])


def _experts(h2t, ub, vtb, s_all, ec_all, x1, gf, tm=512):
    d, n = h2t.shape
    e = ub.shape[0]
    nh2 = s_all.shape[0]
    tm = min(tm, n)
    eblk = 4 * PEER_NKEYS
    nblk = e // eblk
    assert nblk % 2 == 0
    big = pl.BlockSpec((nh2, PEER_NKEYS, tm), lambda i, j: (0, 0, i))
    ublk = lambda f: pl.BlockSpec((eblk, d), lambda i, j: (f(j), 0))
    return pl.pallas_call(
        functools.partial(_experts_body, tm=tm, eblk=eblk, nh=nh2 // 2),
        grid=(n // tm, nblk // 2),
        in_specs=[
            pl.BlockSpec((d, tm), lambda i, j: (0, i)),
            ublk(lambda j: 0), ublk(lambda j: 2 * j + 1), ublk(lambda j: jnp.minimum(2 * j + 2, nblk - 1)),
            pl.BlockSpec((d, 2 * eblk), lambda i, j: (0, j)),
            big, big,
            pl.BlockSpec((tm, d), lambda i, j: (i, 0)),
            _resident(),
        ],
        out_specs=pl.BlockSpec((tm, d), lambda i, j: (i, 0)),
        out_shape=jax.ShapeDtypeStruct((n, d), F32),
        scratch_shapes=[pltpu.VMEM((d, tm), F32), pltpu.VMEM((2, eblk, tm), F32), pltpu.VMEM((eblk, tm), F32)]
                       + [pltpu.VMEM((eblk, tm), BF16)] * 2,
        compiler_params=_params(("arbitrary", "arbitrary")),
        name="peer_experts",
    )(h2t, ub, ub, ub, vtb, s_all, ec_all, x1, gf)


def _arrange_w_in(w, c, d):
    na = N_GROUPS * ATTN_OUT
    o = 3 * c
    k = w[:, o + na:o + 2 * na]
    v = w[:, o + 2 * na:o + 3 * na]
    kv = []
    for g in range(N_GROUPS):
        kv += [k[:, g * ATTN_OUT:(g + 1) * ATTN_OUT], v[:, g * ATTN_OUT:(g + 1) * ATTN_OUT]]
    return jnp.concatenate([w[:, :o + na]] + kv + [w[:, o + 3 * na:]], axis=1).astype(BF16)


def _tail(attn, gya, sgb, x, wts):
    x1, h2, h2t = _mix(attn, gya, sgb, x, wts["wob"], wts["wo"], wts["g2"])
    s_all, ec_all = _route(h2, wts["wq"], wts["keys"])
    return _experts(h2t, wts["ub"], wts["vtb"], s_all, ec_all, x1, wts["gf"])


def kernel(x_prompt, x_sample, cache_kv_w128, cache_kv_w512, cache_kv_w2048, state_conv, norm1_g, w_in,
           conv_w, w_out_a, w_out_b, w_o, norm2_g, peer_wq, peer_keys, peer_u, peer_v, final_g):
    depth = w_in.shape[0]
    assert depth == 1, "single-layer trunk"
    b, s, d = x_prompt.shape
    nb, nt, _ = x_sample.shape
    c = conv_w.shape[-1]
    nh = peer_keys.shape[2]

    ub, vtb = _prep_tables(peer_u[0], peer_v[0])
    wts = dict(
        wob=w_out_b[0].astype(BF16), wo=w_o[0].astype(BF16), g2=norm2_g[0][None, :],
        wq=peer_wq[0].astype(BF16),
        keys=jnp.swapaxes(peer_keys[0], 0, 1).reshape(2 * nh, PEER_NKEYS, -1).astype(BF16),
        ub=ub, vtb=vtb, gf=final_g[None, :],
    )
    win = _arrange_w_in(w_in[0], c, d)
    woa = w_out_a[0].astype(BF16)
    g1 = norm1_g[0][None, :]
    cw = conv_w[0]

    q, kv0, kv1, kv2, gya, sgb, cst_p = _proj_prompt(x_prompt, g1, win, cw, woa)
    kvs = (kv0, kv1, kv2)
    attn = []
    for g in range(N_GROUPS):
        attn += list(_attn_prompt_group(q, kvs[g], g))
    n = b * s
    y_prompt = _tail(attn, gya.reshape(n, d), sgb.reshape(n, d), x_prompt.reshape(n, d), wts).reshape(b, s, d)
    to_rows_major = lambda a: jnp.transpose(a, (0, 4, 1, 2, 3))[None]
    kv_prompt = []
    for g, (win_g, _) in enumerate(GROUPS):
        keep = min(win_g, s)
        kt = _kv_transposed(kvs[g], keep)
        kv_prompt.append(to_rows_major(kt.reshape(b, 2, N_SLOTS, HEAD_DIM, keep)))

    x_tm = jnp.swapaxes(x_sample, 0, 1).reshape(nt * nb, d)
    st_tm = jnp.swapaxes(state_conv[0], 0, 1).reshape((CONV_WIDTH - 1) * nb, c)
    gya_s, sgb_s, cst_s, cols = _proj_sample(x_tm, x_sample.reshape(nb * nt, d), st_tm, g1, win, cw, woa, nb, nt)
    caches_t = [jnp.transpose(cv[0], (0, 2, 3, 4, 1)) for cv in (cache_kv_w128, cache_kv_w512, cache_kv_w2048)]
    new_caches, att = _sample_attention(cols, caches_t, nb, nt)
    ob_s = jnp.swapaxes(att[:, :nt, :], 0, 1).reshape(nt * nb, ATTN_OUT)
    y_s = _tail([ob_s], gya_s, sgb_s, x_tm, wts)
    y_sample = jnp.swapaxes(y_s.reshape(nt, nb, d), 0, 1)
    kv_sample = [to_rows_major(o) for o in new_caches]
    conv_sample = jnp.swapaxes(cst_s.reshape(CONV_WIDTH - 1, nb, c), 0, 1)[None]

    return (y_prompt, y_sample, kv_prompt[0], kv_prompt[1], kv_prompt[2], cst_p[None],
            kv_sample[0], kv_sample[1], kv_sample[2], conv_sample)
```

```python
import functools

import jax
import jax.numpy as jnp
import numpy as np
from jax import lax
from jax.experimental import pallas as pl
from jax.experimental.pallas import tpu as pltpu

F32 = jnp.float32
BF16 = jnp.bfloat16

HEAD_DIM = 64
N_SLOTS = 8
GROUPS = ((128, 1), (512, 4), (2048, 16))
N_GROUPS = len(GROUPS)
ATTN_OUT = N_SLOTS * HEAD_DIM
WIN = 128
CONV_WIDTH = 3
PEER_TOPK = 16
PEER_NKEYS = 128
EPS = 1e-6
NEG = -1e30

LANES = 128
VMEM_LIMIT = 56 * 1024 * 1024

_NT = (((1,), (1,)), ((), ()))


def _params(sem):
    return pltpu.CompilerParams(dimension_semantics=sem, vmem_limit_bytes=VMEM_LIMIT)


def _resident():
    return pl.BlockSpec(memory_space=pltpu.VMEM)


def _rms(x, g):
    return x * lax.rsqrt(jnp.mean(x * x, axis=-1, keepdims=True) + EPS) * g


def _slopes():
    return np.exp2(-8.0 * np.arange(1, N_SLOTS + 1, dtype=np.float64) / N_SLOTS)


def _prep_body(u_ref, v_ref, ub_ref, vt_ref):
    ub_ref[...] = u_ref[...].astype(BF16)
    vt_ref[...] = v_ref[...].T.astype(BF16)


def _prep_tables(u, v):
    e, d = u.shape
    blk = 512
    return pl.pallas_call(
        _prep_body,
        grid=(e // blk,),
        in_specs=[pl.BlockSpec((blk, d), lambda i: (i, 0)), pl.BlockSpec((blk, d), lambda i: (i, 0))],
        out_specs=[pl.BlockSpec((blk, d), lambda i: (i, 0)), pl.BlockSpec((d, blk), lambda i: (0, i))],
        out_shape=[jax.ShapeDtypeStruct((e, d), BF16), jax.ShapeDtypeStruct((d, e), BF16)],
        compiler_params=_params(("arbitrary",)),
        name="prep_tables",
    )(u, v)


def _proj_core(h, win_ref, woa_ref, conv_fn, c, d, q_ref, kv_refs, gya_ref, sgb_ref):
    def proj(a, n):
        return jnp.dot(h, win_ref[:, a:a + n], preferred_element_type=F32)

    o_q = 3 * c
    o_kv = o_q + N_GROUPS * ATTN_OUT
    o_ga = o_kv + N_GROUPS * 2 * ATTN_OUT
    o_gb = o_ga + d
    u = proj(c, c) * proj(2 * c, c)
    y = conv_fn(u)
    ya = jnp.dot((proj(0, c) * y).astype(BF16), woa_ref[...], preferred_element_type=F32)
    gya_ref[...] = jax.nn.sigmoid(proj(o_ga, d)) * ya
    sgb_ref[...] = jax.nn.sigmoid(proj(o_gb, d))
    if q_ref is not None:
        q_ref[...] = proj(o_q, N_GROUPS * ATTN_OUT) * (HEAD_DIM ** -0.5)
        for g in range(N_GROUPS):
            kv_refs[g][...] = proj(o_kv + g * 2 * ATTN_OUT, 2 * ATTN_OUT)


def _proj_prompt_body(x_ref, g1_ref, win_ref, cw_ref, woa_ref,
                      q_ref, kv0_ref, kv1_ref, kv2_ref, gya_ref, sgb_ref, cst_ref, uext_ref, *, tm, c, d):
    j = pl.program_id(1)
    pre = 8

    @pl.when(j == 0)
    def _():
        uext_ref[0:pre, :] = jnp.zeros((pre, c), F32)

    def conv(u):
        uext_ref[pre:pre + tm, :] = u
        cw = cw_ref[...]
        return (cw[0:1] * uext_ref[pre - 2:pre - 2 + tm, :]
                + cw[1:2] * uext_ref[pre - 1:pre - 1 + tm, :]
                + cw[2:3] * u)

    h = _rms(x_ref[...], g1_ref[...]).astype(BF16)
    _proj_core(h, win_ref, woa_ref, conv, c, d, q_ref, (kv0_ref, kv1_ref, kv2_ref), gya_ref, sgb_ref)
    cst_ref[...] = uext_ref[pre + tm - 2:pre + tm, :]
    uext_ref[0:pre, :] = uext_ref[tm:tm + pre, :]


def _proj_prompt(x, g1, win, cw, woa, tm=256):
    b, s, d = x.shape
    c = cw.shape[1]
    nq = N_GROUPS * ATTN_OUT
    tok = lambda n: pl.BlockSpec((None, tm, n), lambda i, j: (i, j, 0))
    outs = [jax.ShapeDtypeStruct((b, s, nq), F32)]
    outs += [jax.ShapeDtypeStruct((b, s, 2 * ATTN_OUT), F32)] * N_GROUPS
    outs += [jax.ShapeDtypeStruct((b, s, d), F32)] * 2
    outs += [jax.ShapeDtypeStruct((b, CONV_WIDTH - 1, c), F32)]
    return pl.pallas_call(
        functools.partial(_proj_prompt_body, tm=tm, c=c, d=d),
        grid=(b, s // tm),
        in_specs=[tok(d), _resident(), _resident(), _resident(), _resident()],
        out_specs=[tok(nq)] + [tok(2 * ATTN_OUT)] * N_GROUPS + [tok(d), tok(d),
                   pl.BlockSpec((None, CONV_WIDTH - 1, c), lambda i, j: (i, 0, 0))],
        out_shape=outs,
        scratch_shapes=[pltpu.VMEM((tm + 8, c), F32)],
        compiler_params=_params(("arbitrary", "arbitrary")),
        name="proj_prompt",
    )(x, g1, win, cw, woa)


def _proj_sample_body(x_ref, xb_ref, st_ref, g1_ref, win_ref, cw_ref, woa_ref,
                      gya_ref, sgb_ref, cst_ref, cols_ref, uext_ref, *, nb, nt, c, d):
    npre = (CONV_WIDTH - 1) * nb
    o_q = 3 * c
    nqkv = 3 * N_GROUPS * ATTN_OUT
    hb = _rms(xb_ref[...], g1_ref[...]).astype(BF16)
    qkv = jnp.dot(hb, win_ref[:, o_q:o_q + nqkv], preferred_element_type=F32)
    nq = N_GROUPS * ATTN_OUT
    cols_ref[0:nq, :] = (qkv[:, :nq] * (HEAD_DIM ** -0.5)).T
    cols_ref[nq:, :] = qkv[:, nq:].T

    def conv(u):
        uext_ref[0:npre, :] = st_ref[...]
        uext_ref[npre:npre + nt * nb, :] = u
        cw = cw_ref[...]
        return (cw[0:1] * uext_ref[0:nt * nb, :]
                + cw[1:2] * uext_ref[nb:nb + nt * nb, :]
                + cw[2:3] * u)

    h = _rms(x_ref[...], g1_ref[...]).astype(BF16)
    _proj_core(h, win_ref, woa_ref, conv, c, d, None, None, gya_ref, sgb_ref)
    cst_ref[...] = uext_ref[nt * nb:nt * nb + npre, :]


def _proj_sample(x_tm, x_bm, st_tm, g1, win, cw, woa, nb, nt):
    n, d = x_tm.shape
    c = cw.shape[1]
    npre = (CONV_WIDTH - 1) * nb
    outs = [jax.ShapeDtypeStruct((n, d), F32)] * 2
    outs += [jax.ShapeDtypeStruct((npre, c), F32)]
    outs += [jax.ShapeDtypeStruct((3 * N_GROUPS * ATTN_OUT, n), F32)]
    return pl.pallas_call(
        functools.partial(_proj_sample_body, nb=nb, nt=nt, c=c, d=d),
        in_specs=[_resident()] * 7,
        out_specs=[_resident()] * 4,
        out_shape=outs,
        scratch_shapes=[pltpu.VMEM((npre + n, c), F32)],
        compiler_params=pltpu.CompilerParams(vmem_limit_bytes=VMEM_LIMIT),
        name="proj_sample",
    )(x_tm, x_bm, st_tm, g1, win, cw, woa)


def _kvt_body(kv_ref, out_ref):
    out_ref[...] = kv_ref[...].T


def _kv_transposed(kv, keep, tm=256):
    b, s, w = kv.shape
    tm = min(tm, keep)
    off = (s - keep) // tm
    return pl.pallas_call(
        _kvt_body,
        grid=(b, keep // tm),
        in_specs=[pl.BlockSpec((None, tm, w), lambda i, j: (i, j + off, 0))],
        out_specs=pl.BlockSpec((None, w, tm), lambda i, j: (i, 0, j)),
        out_shape=jax.ShapeDtypeStruct((b, w, keep), F32),
        compiler_params=_params(("arbitrary", "arbitrary")),
        name="kv_transposed",
    )(kv)


def _prompt_bias(dil):
    tq = np.arange(WIN)[:, None]
    col = np.arange(2 * WIN)[None, :]
    dist = tq + WIN - col
    ok = (dist >= 0) & (dist <= WIN)
    b = -_slopes()[:, None, None] * (dil * dist)[None]
    return jnp.asarray(np.where(ok[None], b, NEG), F32)


def _attn_prompt_body(*refs, dil, npair, has_prev):
    if has_prev:
        q_ref, k_ref, v_ref, kp_ref, vp_ref, bias_ref, acc_ref, m_ref, l_ref = refs
    else:
        q_ref, k_ref, v_ref, bias_ref, acc_ref, m_ref, l_ref = refs
        kp_ref, vp_ref = k_ref, v_ref
    first = pl.program_id(2) == 0
    pair0 = pl.program_id(1) * npair
    lo = lax.broadcasted_iota(jnp.int32, (WIN, LANES), 1) < HEAD_DIM
    col = lax.broadcasted_iota(jnp.int32, (WIN, 2 * WIN), 1)
    kill = jnp.logical_and(col < WIN, first)
    for p in range(npair):
        ls = slice(p * LANES, (p + 1) * LANES)
        for r in range(dil):
            rows = slice(None) if dil == 1 else pl.ds(r, WIN, stride=dil)
            qp = q_ref[rows, ls]
            kcat = jnp.concatenate([kp_ref[rows, ls], k_ref[rows, ls]], axis=0).astype(BF16)
            vcat = jnp.concatenate([vp_ref[rows, ls], v_ref[rows, ls]], axis=0).astype(BF16)
            res = []
            for hh in range(2):
                qm = jnp.where(lo if hh == 0 else jnp.logical_not(lo), qp, 0.0).astype(BF16)
                s = lax.dot_general(qm, kcat, _NT, preferred_element_type=F32)
                s = jnp.where(kill, NEG, s + bias_ref[2 * (pair0 + p) + hh])
                m = jnp.max(s, axis=-1, keepdims=True)
                e = jnp.exp(s - m)
                l = jnp.sum(e, axis=-1, keepdims=True)
                pv = jnp.dot(e.astype(BF16), vcat, preferred_element_type=F32)
                res.append((m, l, pv))
            acc_ref[rows, ls] = jnp.where(lo, res[0][2], res[1][2])
            m_ref[rows, ls] = jnp.where(lo, res[0][0], res[1][0])
            l_ref[rows, ls] = jnp.where(lo, res[0][1], res[1][1])


def _attn_prompt_group(q, kv, g):
    b, s, nq = q.shape
    dil = GROUPS[g][1]
    span = WIN * dil
    nspan = s // span
    has_prev = nspan > 1
    npair = 4 if dil == 1 else 1
    w = npair * LANES
    nw = ATTN_OUT // w
    cur = lambda off: pl.BlockSpec((None, span, w), lambda i, p, j: (i, j, off + p))
    prev = lambda off: pl.BlockSpec((None, span, w), lambda i, p, j: (i, jnp.maximum(j - 1, 0), off + p))
    in_specs = [cur(g * nw), cur(0), cur(nw)] + ([prev(0), prev(nw)] if has_prev else []) + [_resident()]
    out_sd = jax.ShapeDtypeStruct((b, s, ATTN_OUT), F32)
    acc, m, l = pl.pallas_call(
        functools.partial(_attn_prompt_body, dil=dil, npair=npair, has_prev=has_prev),
        grid=(b, nw, nspan),
        in_specs=in_specs,
        out_specs=[cur(0)] * 3,
        out_shape=[out_sd] * 3,
        compiler_params=_params(("arbitrary", "arbitrary", "arbitrary")),
        name=f"attn_prompt_g{g}",
    )(q, kv, kv, *([kv, kv] if has_prev else []), _prompt_bias(dil))
    shp = (b * s, ATTN_OUT)
    return acc.reshape(shp), m.reshape(shp), l.reshape(shp)


def _sample_bias(nt):
    sl = _slopes()[:, None, None]
    cache, new = [], []
    for win, dil in GROUPS:
        t = np.arange(8)[None, :, None]
        dist = win + t - np.arange(win)[None, None, :]
        ok = (dist % dil == 0) & (dist <= win)
        cache.append(jnp.asarray(np.where(t < nt, np.where(ok, -sl * dist, NEG), 0.0), F32))
        dn = t - np.arange(LANES)[None, None, :]
        okn = (dn >= 0) & (dn % dil == 0)
        new.append(np.where(t < nt, np.where(okn, -sl * dn, NEG), 0.0))
    return cache, jnp.asarray(np.stack(new), F32)


def _sample_body(cols_ref, c0_ref, c1_ref, c2_ref, b0_ref, b1_ref, b2_ref, bn_ref,
                 o0_ref, o1_ref, o2_ref, att_ref, *, nt, hc):
    b = pl.program_id(0)
    col0 = b * nt
    tile = pl.ds(pl.multiple_of((col0 // LANES) * LANES, LANES), LANES)
    to_lane0 = (LANES - col0 % LANES) % LANES
    lane = lax.broadcasted_iota(jnp.int32, (HEAD_DIM, LANES), 1)
    keep = lane < LANES - nt
    c_refs = (c0_ref, c1_ref, c2_ref)
    o_refs = (o0_ref, o1_ref, o2_ref)
    bias_refs = (b0_ref, b1_ref, b2_ref)
    nq = N_GROUPS * ATTN_OUT

    outs = []
    for hl in range(hc):
        h = pl.program_id(1) * hc + hl
        stats = []
        for g in range(N_GROUPS):
            wb = c_refs[g].shape[-1]
            ntile = wb // LANES

            def cols(row0):
                rows = pl.ds(pl.multiple_of(row0 + h * HEAD_DIM, HEAD_DIM), HEAD_DIM)
                return pltpu.roll(cols_ref[rows, tile], to_lane0, axis=1)

            q_t = cols(g * ATTN_OUT)
            q8 = jnp.concatenate([q_t, jnp.zeros_like(q_t)], axis=0).T[0:8, 0:HEAD_DIM].astype(BF16)
            new = (cols(nq + g * 2 * ATTN_OUT), cols(nq + g * 2 * ATTN_OUT + ATTN_OUT))
            old = (c_refs[g][0, hl], c_refs[g][1, hl])
            for kv in range(2):
                rot = [pltpu.roll(old[kv][:, j * LANES:(j + 1) * LANES], LANES - nt, axis=1) for j in range(ntile)]
                rot.append(pltpu.roll(new[kv], LANES - nt, axis=1))
                for j in range(ntile):
                    o_refs[g][kv, hl, :, j * LANES:(j + 1) * LANES] = jnp.where(keep, rot[j], rot[j + 1])
            s = jnp.dot(q8, old[0].astype(BF16), preferred_element_type=F32) + bias_refs[g][h]
            sn = jnp.dot(q8, new[0].astype(BF16), preferred_element_type=F32) + bn_ref[g, h]
            m = jnp.maximum(jnp.max(s, axis=1, keepdims=True), jnp.max(sn, axis=1, keepdims=True))
            e = jnp.exp(s - m)
            en = jnp.exp(sn - m)
            l = jnp.sum(e, axis=1, keepdims=True) + jnp.sum(en, axis=1, keepdims=True)
            acc = (lax.dot_general(e.astype(BF16), old[1].astype(BF16), _NT, preferred_element_type=F32)
                   + lax.dot_general(en.astype(BF16), new[1].astype(BF16), _NT, preferred_element_type=F32))
            stats.append((m, l, acc))
        mm = functools.reduce(jnp.maximum, [st[0] for st in stats])
        num = 0.0
        den = 0.0
        for m, l, acc in stats:
            w = jnp.exp(m - mm)
            num = num + w * acc
            den = den + w * l
        outs.append(num / den)
    att_ref[...] = jnp.concatenate(outs, axis=1)


def _sample_attention(cols, caches_t, nb, nt, hc=4):
    cbias, nbias = _sample_bias(nt)
    for g, (win, _) in enumerate(GROUPS):
        assert caches_t[g].shape[-1] == win, "window buffer must hold a full window"
    assert nt <= 8 and LANES % nt == 0 and (nb * nt) % LANES == 0
    cspec = lambda wb: pl.BlockSpec((None, 2, hc, HEAD_DIM, wb), lambda i, j: (i, 0, j, 0, 0))
    cspecs = [cspec(c.shape[-1]) for c in caches_t]
    res = pl.pallas_call(
        functools.partial(_sample_body, nt=nt, hc=hc),
        grid=(nb, N_SLOTS // hc),
        in_specs=[_resident()] + cspecs + [_resident()] * (N_GROUPS + 1),
        out_specs=cspecs + [pl.BlockSpec((None, 8, hc * HEAD_DIM), lambda i, j: (i, 0, j))],
        out_shape=[jax.ShapeDtypeStruct(c.shape, F32) for c in caches_t]
                  + [jax.ShapeDtypeStruct((nb, 8, ATTN_OUT), F32)],
        compiler_params=_params(("arbitrary", "arbitrary")),
        name="sample_attention",
    )(cols, *caches_t, *cbias, nbias)
    return res[:N_GROUPS], res[N_GROUPS]


def _mix_body(*refs, combine):
    if combine:
        stats = [refs[3 * g:3 * g + 3] for g in range(N_GROUPS)]
        rest = refs[3 * N_GROUPS:]
        mm = functools.reduce(jnp.maximum, [st[1][...] for st in stats])
        num = 0.0
        den = 0.0
        for a_ref, m_ref, l_ref in stats:
            w = jnp.exp(m_ref[...] - mm)
            num = num + w * a_ref[...]
            den = den + w * l_ref[...]
        ob = num / den
    else:
        ob = refs[0][...]
        rest = refs[1:]
    gya_ref, sgb_ref, x_ref, wob_ref, wo_ref, g2_ref, x1_ref, h2_ref, h2t_ref = rest
    yb = jnp.dot(ob.astype(BF16), wob_ref[...], preferred_element_type=F32)
    mix = gya_ref[...] + sgb_ref[...] * yb
    x1 = x_ref[...] + jnp.dot(mix.astype(BF16), wo_ref[...], preferred_element_type=F32)
    x1_ref[...] = x1
    h2 = _rms(x1, g2_ref[...])
    h2_ref[...] = h2.astype(BF16)
    h2t_ref[...] = h2.T.astype(BF16)


def _mix(attn, gya, sgb, x, wob, wo, g2, tm=512):
    n, d = x.shape
    tm = min(tm, n)
    combine = len(attn) > 1
    tok = lambda w: pl.BlockSpec((tm, w), lambda i: (i, 0))
    return pl.pallas_call(
        functools.partial(_mix_body, combine=combine),
        grid=(n // tm,),
        in_specs=[tok(ATTN_OUT)] * len(attn) + [tok(d), tok(d), tok(d), _resident(), _resident(), _resident()],
        out_specs=[tok(d), tok(d), pl.BlockSpec((d, tm), lambda i: (0, i))],
        out_shape=[jax.ShapeDtypeStruct((n, d), F32), jax.ShapeDtypeStruct((n, d), BF16),
                   jax.ShapeDtypeStruct((d, n), BF16)],
        compiler_params=_params(("arbitrary",)),
        name="mix_combine" if combine else "mix",
    )(*attn, gya, sgb, x, wob, wo, g2)


def _sort16_pairs():
    n, pairs, p = PEER_TOPK, [], 1
    while p < n:
        k = p
        while k >= 1:
            for j in range(k % p, n - k, 2 * k):
                for i in range(min(k, n - j - k)):
                    if (i + j) // (2 * p) == (i + j + k) // (2 * p):
                        pairs.append((i + j, i + j + k))
            k //= 2
        p *= 2
    return pairs


def _exchange(x, i, j):
    x[i], x[j] = jnp.maximum(x[i], x[j]), jnp.minimum(x[i], x[j])


def _merge_top16(a, b):
    n = PEER_TOPK
    c = [jnp.maximum(a[i], b[n - 1 - i]) for i in range(n)]
    d = n // 2
    while d >= 1:
        for i in range(n):
            if i & d == 0:
                _exchange(c, i, i + d)
        d //= 2
    return c


def _merge_sublanes(x):
    for shift in (4, 2, 1):
        x = _merge_top16(x, [pltpu.roll(v, shift, axis=0) for v in x])
    return x


def _route_body(h2_ref, wq_ref, keys_ref, s_ref, ec_ref, vals_ref, *, tm, nh):
    q = jnp.dot(h2_ref[...], wq_ref[...], preferred_element_type=F32).astype(BF16)
    for hh in range(2 * nh):
        s_ref[hh] = lax.dot_general(keys_ref[hh], q[:, hh * PEER_NKEYS:(hh + 1) * PEER_NKEYS], _NT,
                                    preferred_element_type=F32)
    nchunk = tm // LANES
    pairs = _sort16_pairs()

    def top_keys(i, carry):
        hh = i // nchunk
        cs = pl.ds(pl.multiple_of((i % nchunk) * LANES, LANES), LANES)
        x = [s_ref[hh, 8 * k:8 * k + 8, cs] for k in range(PEER_NKEYS // 8)]
        for a, b in pairs:
            _exchange(x, a, b)
        x = _merge_sublanes(x)
        for k in range(PEER_TOPK):
            vals_ref[hh, k, :, cs] = x[k]
        return carry

    lax.fori_loop(0, 2 * nh * nchunk, top_keys, 0)

    r8 = lax.broadcasted_iota(jnp.int32, (8, LANES), 0)
    count = functools.reduce(lambda acc, a: jnp.where(r8 == a, PEER_TOPK // (a + 1), acc), range(8), r8 * 0)

    def top_pairs(i, carry):
        h = i // nchunk
        cs = pl.ds(pl.multiple_of((i % nchunk) * LANES, LANES), LANES)
        v1 = [vals_ref[2 * h, k, :, cs] for k in range(PEER_TOPK)]
        v2 = [vals_ref[2 * h + 1, k, :, cs] for k in range(PEER_TOPK)]
        v1col = functools.reduce(lambda acc, a: jnp.where(r8 == a, v1[a], acc), range(1, 8), v1[0])
        x = [jnp.where(count > k, v1col + v2[k], -jnp.inf) for k in range(PEER_TOPK)]
        x = _merge_sublanes(x)
        tail = [v1[8 + k] + v2[0] if k < 8 else jnp.full((8, LANES), -jnp.inf, F32) for k in range(PEER_TOPK)]
        sv = _merge_top16(x, tail)
        z = functools.reduce(lambda a, b: a + b, [jnp.exp(v - sv[0]) for v in sv])[0:1]
        tau = sv[PEER_TOPK - 1]
        ec_ref[2 * h, :, cs] = jnp.exp(s_ref[2 * h, :, cs] - v1[0][0:1]) / z
        ec_ref[2 * h + 1, :, cs] = jnp.exp(s_ref[2 * h + 1, :, cs] - v2[0][0:1])
        for k in range(PEER_NKEYS // 8):
            rows = slice(8 * k, 8 * k + 8)
            s1 = s_ref[2 * h, rows, cs]
            th = jnp.full((8, LANES), jnp.inf, F32)
            for b in range(PEER_TOPK):
                th = jnp.where(s1 + v2[b] >= tau, v2[b], th)
            s_ref[2 * h, rows, cs] = th
        return carry

    lax.fori_loop(0, nh * nchunk, top_pairs, 0)


def _route(h2, wq, keys, tm=512):
    n, d = h2.shape
    nh2 = keys.shape[0]
    tm = min(tm, n)
    big = pl.BlockSpec((nh2, PEER_NKEYS, tm), lambda i: (0, 0, i))
    return pl.pallas_call(
        functools.partial(_route_body, tm=tm, nh=nh2 // 2),
        grid=(n // tm,),
        in_specs=[pl.BlockSpec((tm, d), lambda i: (i, 0)), _resident(), _resident()],
        out_specs=[big, big],
        out_shape=[jax.ShapeDtypeStruct((nh2, PEER_NKEYS, n), F32)] * 2,
        scratch_shapes=[pltpu.VMEM((nh2, PEER_TOPK, 8, tm), F32)],
        compiler_params=_params(("arbitrary",)),
        name="peer_route",
    )(h2, wq, keys)


def _gate_half(grp, col0, act_ref, p_ref, s_ref, ec_ref, *, half, eblk, nh):
    rows = PEER_NKEYS // 2
    for sub in range(eblk // PEER_NKEYS):
        for c0 in range(0, half, LANES):
            cs = slice(col0 + c0, col0 + c0 + LANES)
            for k0 in range(0, PEER_NKEYS, rows):
                ks = slice(k0, k0 + rows)
                g = jnp.zeros((rows, LANES), F32)
                for h in range(nh):
                    th = s_ref[2 * h, grp, cs][sub:sub + 1]
                    c1 = ec_ref[2 * h, grp, cs][sub:sub + 1]
                    w = ec_ref[2 * h + 1, ks, cs] * c1
                    g = g + jnp.where(s_ref[2 * h + 1, ks, cs] >= th, w, 0.0)
                rs = slice(sub * PEER_NKEYS + k0, sub * PEER_NKEYS + k0 + rows)
                a = act_ref[rs, c0:c0 + LANES]
                gelu = 0.5 * a * (1.0 + lax.erf(a * np.float32(np.sqrt(0.5))))
                p_ref[rs, c0:c0 + LANES] = (g * gelu).astype(BF16)


def _experts_body(h2t_ref, u_ref, vt_ref, s_ref, ec_ref, x1_ref, gf_ref, y_ref,
                  acc_ref, act_ref, p_ref, *, tm, eblk, nh):
    j = pl.program_id(1)
    grp = pl.ds(pl.multiple_of(j * 8, 8), 8)

    @pl.when(j == 0)
    def _():
        acc_ref[...] = jnp.zeros_like(acc_ref)

    act_ref[...] = jnp.dot(u_ref[...], h2t_ref[...], preferred_element_type=F32)
    _gate_half(grp, 0, act_ref, p_ref, s_ref, ec_ref, half=tm, eblk=eblk, nh=nh)
    acc_ref[...] += jnp.dot(vt_ref[...], p_ref[...], preferred_element_type=F32)

    @pl.when(j == pl.num_programs(1) - 1)
    def _():
        y_ref[...] = _rms(x1_ref[...] + acc_ref[...].T, gf_ref[...])


def _experts(h2t, ub, vtb, s_all, ec_all, x1, gf, tm=512):
    d, n = h2t.shape
    e = ub.shape[0]
    nh2 = s_all.shape[0]
    tm = min(tm, n)
    eblk = 8 * PEER_NKEYS
    big = pl.BlockSpec((nh2, PEER_NKEYS, tm), lambda i, j: (0, 0, i))
    return pl.pallas_call(
        functools.partial(_experts_body, tm=tm, eblk=eblk, nh=nh2 // 2),
        grid=(n // tm, e // eblk),
        in_specs=[
            pl.BlockSpec((d, tm), lambda i, j: (0, i)),
            pl.BlockSpec((eblk, d), lambda i, j: (j, 0)),
            pl.BlockSpec((d, eblk), lambda i, j: (0, j)),
            big, big,
            pl.BlockSpec((tm, d), lambda i, j: (i, 0)),
            _resident(),
        ],
        out_specs=pl.BlockSpec((tm, d), lambda i, j: (i, 0)),
        out_shape=jax.ShapeDtypeStruct((n, d), F32),
        scratch_shapes=[pltpu.VMEM((d, tm), F32), pltpu.VMEM((eblk, tm), F32), pltpu.VMEM((eblk, tm), BF16)],
        compiler_params=_params(("arbitrary", "arbitrary")),
        name="peer_experts",
    )(h2t, ub, vtb, s_all, ec_all, x1, gf)


def _arrange_w_in(w, c, d):
    na = N_GROUPS * ATTN_OUT
    o = 3 * c
    k = w[:, o + na:o + 2 * na]
    v = w[:, o + 2 * na:o + 3 * na]
    kv = []
    for g in range(N_GROUPS):
        kv += [k[:, g * ATTN_OUT:(g + 1) * ATTN_OUT], v[:, g * ATTN_OUT:(g + 1) * ATTN_OUT]]
    return jnp.concatenate([w[:, :o + na]] + kv + [w[:, o + 3 * na:]], axis=1).astype(BF16)


def _tail(attn, gya, sgb, x, wts):
    x1, h2, h2t = _mix(attn, gya, sgb, x, wts["wob"], wts["wo"], wts["g2"])
    s_all, ec_all = _route(h2, wts["wq"], wts["keys"])
    return _experts(h2t, wts["ub"], wts["vtb"], s_all, ec_all, x1, wts["gf"])


def kernel(x_prompt, x_sample, cache_kv_w128, cache_kv_w512, cache_kv_w2048, state_conv, norm1_g, w_in,
           conv_w, w_out_a, w_out_b, w_o, norm2_g, peer_wq, peer_keys, peer_u, peer_v, final_g):
    depth = w_in.shape[0]
    assert depth == 1, "single-layer trunk"
    b, s, d = x_prompt.shape
    nb, nt, _ = x_sample.shape
    c = conv_w.shape[-1]
    nh = peer_keys.shape[2]

    ub, vtb = _prep_tables(peer_u[0], peer_v[0])
    wts = dict(
        wob=w_out_b[0].astype(BF16), wo=w_o[0].astype(BF16), g2=norm2_g[0][None, :],
        wq=peer_wq[0].astype(BF16),
        keys=jnp.swapaxes(peer_keys[0], 0, 1).reshape(2 * nh, PEER_NKEYS, -1).astype(BF16),
        ub=ub, vtb=vtb, gf=final_g[None, :],
    )
    win = _arrange_w_in(w_in[0], c, d)
    woa = w_out_a[0].astype(BF16)
    g1 = norm1_g[0][None, :]
    cw = conv_w[0]

    q, kv0, kv1, kv2, gya, sgb, cst_p = _proj_prompt(x_prompt, g1, win, cw, woa)
    kvs = (kv0, kv1, kv2)
    attn = []
    for g in range(N_GROUPS):
        attn += list(_attn_prompt_group(q, kvs[g], g))
    n = b * s
    y_prompt = _tail(attn, gya.reshape(n, d), sgb.reshape(n, d), x_prompt.reshape(n, d), wts).reshape(b, s, d)
    to_rows_major = lambda a: jnp.transpose(a, (0, 4, 1, 2, 3))[None]
    kv_prompt = []
    for g, (win_g, _) in enumerate(GROUPS):
        keep = min(win_g, s)
        kt = _kv_transposed(kvs[g], keep)
        kv_prompt.append(to_rows_major(kt.reshape(b, 2, N_SLOTS, HEAD_DIM, keep)))

    x_tm = jnp.swapaxes(x_sample, 0, 1).reshape(nt * nb, d)
    st_tm = jnp.swapaxes(state_conv[0], 0, 1).reshape((CONV_WIDTH - 1) * nb, c)
    gya_s, sgb_s, cst_s, cols = _proj_sample(x_tm, x_sample.reshape(nb * nt, d), st_tm, g1, win, cw, woa, nb, nt)
    caches_t = [jnp.transpose(cv[0], (0, 2, 3, 4, 1)) for cv in (cache_kv_w128, cache_kv_w512, cache_kv_w2048)]
    new_caches, att = _sample_attention(cols, caches_t, nb, nt)
    ob_s = jnp.swapaxes(att[:, :nt, :], 0, 1).reshape(nt * nb, ATTN_OUT)
    y_s = _tail([ob_s], gya_s, sgb_s, x_tm, wts)
    y_sample = jnp.swapaxes(y_s.reshape(nt, nb, d), 0, 1)
    kv_sample = [to_rows_major(o) for o in new_caches]
    conv_sample = jnp.swapaxes(cst_s.reshape(CONV_WIDTH - 1, nb, c), 0, 1)[None]

    return (y_prompt, y_sample, kv_prompt[0], kv_prompt[1], kv_prompt[2], cst_p[None],
            kv_sample[0], kv_sample[1], kv_sample[2], conv_sample)
```
